```python
import math
import jax
import jax.numpy as jnp
from jax import lax
import numpy as np

D_MODEL = 1024
BATCH = 4
SEQ = 4096
DEPTH = 4
DEC_BATCH = 32
DEC_SEQ = 32
PAST_LEN = 2048

CHUNK = 64
Q_BLOCK = 128
GATHER_ROWS = 128
N_MIXERS = 3
HEAD_DIM = 64
N_HEADS_A = D_MODEL // HEAD_DIM
N_HEADS_B = D_MODEL // (2 * HEAD_DIM)
N_HEADS_C = D_MODEL // HEAD_DIM
N_IDX_HEADS = 8
IDX_DIM = 64
TOPK_MAX = 256
D_FF = 4 * D_MODEL
ROPE_THETA = 10000.0
NORM_EPS = 1e-6
SUBLN_EPS = 1e-5
ATTN_SCALE = HEAD_DIM ** -0.5
IDX_SCALE = IDX_DIM ** -0.5
IDX_HEAD_SCALE = N_IDX_HEADS ** -0.5
FORGET_BIAS_LO = 2.0
FORGET_BIAS_HI = 6.0
N_A = len(range(0, DEPTH, N_MIXERS))
N_B = len(range(1, DEPTH, N_MIXERS))
N_C = len(range(2, DEPTH, N_MIXERS))

kernel_name = 'hybrid_fox_diff_dsa_stream_step'


def rms_norm(x, g, eps=NORM_EPS):
    xf = x.astype(jnp.float32)
    y = xf * lax.rsqrt(jnp.mean(xf * xf, axis=-1, keepdims=True) + eps)
    return (y * g.astype(jnp.float32)).astype(x.dtype)


def rope(x, pos):
    half = x.shape[-1] // 2
    inv = ROPE_THETA ** (-jnp.arange(half, dtype=jnp.float32) / half)
    ang = pos.astype(jnp.float32)[:, None] * inv[None, :]
    cos = jnp.cos(ang)[None, :, None, :]
    sin = jnp.sin(ang)[None, :, None, :]
    xf = x.astype(jnp.float32)
    x1, x2 = xf[..., :half], xf[..., half:]
    return jnp.concatenate([x1 * cos - x2 * sin, x2 * cos + x1 * sin], axis=-1).astype(x.dtype)


def sweep_query_blocks(fn, q_pos, *q_args):
    t = q_pos.shape[0]
    qb = Q_BLOCK if t % Q_BLOCK == 0 else t
    nb = t // qb
    blocked = tuple(jnp.moveaxis(a.reshape(a.shape[0], nb, qb, *a.shape[2:]), 1, 0) for a in q_args)
    out = lax.map(lambda xs: fn(xs[0], *xs[1]), (q_pos.reshape(nb, qb), blocked))
    out = jnp.moveaxis(out, 0, 1)
    return out.reshape(out.shape[0], t, *out.shape[3:])


def fox_mixer(h, pos, past, w_qkv, w_f, b_f, w_o):
    b, t, _ = h.shape
    q, k, v = jnp.split(h @ w_qkv, 3, axis=-1)
    q = q.reshape(b, t, N_HEADS_A, HEAD_DIM)
    k = k.reshape(b, t, N_HEADS_A, HEAD_DIM)
    v = v.reshape(b, t, N_HEADS_A, HEAD_DIM)
    logf = jax.nn.log_sigmoid((h @ w_f + b_f).astype(jnp.float32))
    if past is None:
        kk, vv, lf, k_pos = k, v, logf, pos
    else:
        pk, pv, plf, ppos = past
        kk = jnp.concatenate([pk.astype(k.dtype), k], axis=1)
        vv = jnp.concatenate([pv.astype(v.dtype), v], axis=1)
        lf = jnp.concatenate([plf.astype(jnp.float32), logf], axis=1)
        k_pos = jnp.concatenate([ppos, pos])
    cum = jnp.cumsum(lf, axis=1)
    cum_k = jnp.moveaxis(cum, 1, 2)

    def block(qpos, qb, cqb):
        logits = jnp.einsum('bqhd,bshd->bhqs', qb, kk).astype(jnp.float32) * ATTN_SCALE
        logits = logits + jnp.moveaxis(cqb, 1, 2)[..., None] - cum_k[:, :, None, :]
        mask = k_pos[None, :] <= qpos[:, None]
        p = jax.nn.softmax(jnp.where(mask, logits, -jnp.inf), axis=-1).astype(vv.dtype)
        return jnp.einsum('bhqs,bshd->bqhd', p, vv)

    o = sweep_query_blocks(block, pos, q, cum[:, -t:])
    y = o.reshape(b, t, D_MODEL) @ w_o
    return y, k, v, logf.astype(h.dtype)


def diff_lambda_init(layer):
    return 0.8 - 0.6 * math.exp(-0.3 * layer)


def diff_mixer(h, pos, past, layer, w_qkv, lam_q1, lam_k1, lam_q2, lam_k2, subln_g, w_o):
    b, t, _ = h.shape
    q, k, v = jnp.split(h @ w_qkv, 3, axis=-1)
    q = rope(q.reshape(b, t, 2 * N_HEADS_B, HEAD_DIM), pos)
    k = rope(k.reshape(b, t, 2 * N_HEADS_B, HEAD_DIM), pos)
    v = v.reshape(b, t, N_HEADS_B, 2 * HEAD_DIM)
    if past is None:
        kk, vv, k_pos = k, v, pos
    else:
        pk, pv, ppos = past
        kk = jnp.concatenate([pk.astype(k.dtype), k], axis=1)
        vv = jnp.concatenate([pv.astype(v.dtype), v], axis=1)
        k_pos = jnp.concatenate([ppos, pos])
    lam_init = diff_lambda_init(layer)
    f32 = jnp.float32
    lam = (jnp.exp(jnp.sum(lam_q1.astype(f32) * lam_k1.astype(f32)))
           - jnp.exp(jnp.sum(lam_q2.astype(f32) * lam_k2.astype(f32))) + lam_init)
    k_chunk = k_pos // CHUNK

    def block(qpos, qb):
        nq = qb.shape[1]
        logits = jnp.einsum('bqgd,bsgd->bgqs', qb, kk).astype(f32) * ATTN_SCALE
        mask = k_chunk[None, :] <= (qpos // CHUNK)[:, None]
        p = jax.nn.softmax(jnp.where(mask, logits, -jnp.inf), axis=-1)
        p = p.reshape(b, N_HEADS_B, 2, nq, -1)
        a = (p[:, :, 0] - lam * p[:, :, 1]).astype(vv.dtype)
        return jnp.einsum('bhqs,bshe->bqhe', a, vv)

    o = sweep_query_blocks(block, pos, q)
    o = rms_norm(o, subln_g, SUBLN_EPS) * (1.0 - lam_init)
    return o.reshape(b, t, D_MODEL) @ w_o, k, v


def dsa_select(qi, wi, ki_all, q_pos, k_pos, top_k):
    k_chunk = k_pos // CHUNK

    def block(qpos, qib, wib):
        s = jnp.einsum('bqid,bsd->bqis', qib, ki_all).astype(jnp.float32) * IDX_SCALE
        score = jnp.einsum('bqis,bqi->bqs', jax.nn.relu(s), wib.astype(jnp.float32))
        adm = k_chunk[None, :] <= (qpos // CHUNK)[:, None]
        score = jnp.where(adm[None], score, -jnp.inf)
        _, idx = lax.top_k(score, top_k)
        return idx

    idx = sweep_query_blocks(block, q_pos, qi, wi)
    ok = k_chunk[idx] <= (q_pos // CHUNK)[None, :, None]
    return idx, ok


def dsa_attend(q, kk, vv, idx, ok):
    b, t, nh, dh = q.shape
    length = kk.shape[1]
    top_k = idx.shape[-1]
    n = b * t
    rows = math.gcd(n, GATHER_ROWS)
    flat_idx = (idx + (jnp.arange(b, dtype=jnp.int32) * length)[:, None, None]).reshape(n // rows, rows, top_k)
    k_flat = kk.reshape(b * length, nh, dh)
    v_flat = vv.reshape(b * length, nh, dh)

    def block(args):
        qb, ib, okb = args
        kg = k_flat[ib]
        vg = v_flat[ib]
        logits = jnp.einsum('rhd,rkhd->rhk', qb, kg).astype(jnp.float32) * ATTN_SCALE
        p = jax.nn.softmax(jnp.where(okb[:, None, :], logits, -jnp.inf), axis=-1).astype(vg.dtype)
        return jnp.einsum('rhk,rkhd->rhd', p, vg)

    out = lax.map(block, (q.reshape(n // rows, rows, nh, dh), flat_idx, ok.reshape(n // rows, rows, top_k)))
    return out.reshape(b, t, nh, dh)


def dsa_mixer(h, pos, past, w_qkv, w_qidx, w_kidx, w_widx, w_o):
    b, t, _ = h.shape
    q, k, v = jnp.split(h @ w_qkv, 3, axis=-1)
    q = rope(q.reshape(b, t, N_HEADS_C, HEAD_DIM), pos)
    k = rope(k.reshape(b, t, N_HEADS_C, HEAD_DIM), pos)
    v = v.reshape(b, t, N_HEADS_C, HEAD_DIM)
    qi = rope((h @ w_qidx).reshape(b, t, N_IDX_HEADS, IDX_DIM), pos)
    ki = rope((h @ w_kidx)[:, :, None, :], pos)[:, :, 0]
    wi = (h @ w_widx) * IDX_HEAD_SCALE
    if past is None:
        kk, vv, kki, k_pos = k, v, ki, pos
    else:
        pk, pv, pki, ppos = past
        kk = jnp.concatenate([pk.astype(k.dtype), k], axis=1)
        vv = jnp.concatenate([pv.astype(v.dtype), v], axis=1)
        kki = jnp.concatenate([pki.astype(ki.dtype), ki], axis=1)
        k_pos = jnp.concatenate([ppos, pos])
    top_k = min(TOPK_MAX, kk.shape[1] // 4)
    idx, ok = dsa_select(qi, wi, kki, pos, k_pos, top_k)
    o = dsa_attend(q, kk, vv, idx, ok)
    return o.reshape(b, t, D_MODEL) @ w_o, k, v, ki


def sq_relu_ffn(h, w_up, w_down):
    return jnp.square(jax.nn.relu(h @ w_up)) @ w_down


def setup_inputs(seed: int = 0) -> dict:
    key = jax.random.key(seed)
    ks = iter(jax.random.split(key, 40))
    f32 = jnp.float32

    def nrm(shape, scale=1.0):
        return jax.random.normal(next(ks), shape, f32) * scale

    def gain(shape):
        return 1.0 + nrm(shape, 0.01)

    d = D_MODEL
    dsc = d ** -0.5
    inputs = {
        'x_prompt': nrm((BATCH, SEQ, d)),
        'x_sample': nrm((DEC_BATCH, DEC_SEQ, d)),
        'cache_a_k': nrm((N_A, DEC_BATCH, PAST_LEN, N_HEADS_A, HEAD_DIM)),
        'cache_a_v': nrm((N_A, DEC_BATCH, PAST_LEN, N_HEADS_A, HEAD_DIM)),
        'cache_a_logf': jax.nn.log_sigmoid(nrm((N_A, DEC_BATCH, PAST_LEN, N_HEADS_A)) + 0.5 * (FORGET_BIAS_LO + FORGET_BIAS_HI)),
        'cache_b_k': nrm((N_B, DEC_BATCH, PAST_LEN, 2 * N_HEADS_B, HEAD_DIM)),
        'cache_b_v': nrm((N_B, DEC_BATCH, PAST_LEN, N_HEADS_B, 2 * HEAD_DIM)),
        'cache_c_k': nrm((N_C, DEC_BATCH, PAST_LEN, N_HEADS_C, HEAD_DIM)),
        'cache_c_v': nrm((N_C, DEC_BATCH, PAST_LEN, N_HEADS_C, HEAD_DIM)),
        'cache_c_kidx': nrm((N_C, DEC_BATCH, PAST_LEN, IDX_DIM)),
        'norm_mix_g': gain((DEPTH, d)),
        'norm_ffn_g': gain((DEPTH, d)),
        'norm_final_g': gain((d,)),
        'a_w_qkv': nrm((N_A, d, 3 * d), dsc),
        'a_w_f': nrm((N_A, d, N_HEADS_A), dsc),
        'a_b_f': jax.random.uniform(next(ks), (N_A, N_HEADS_A), f32, FORGET_BIAS_LO, FORGET_BIAS_HI),
        'a_w_o': nrm((N_A, d, d), dsc),
        'b_w_qkv': nrm((N_B, d, 3 * d), dsc),
        'b_lam_q1': nrm((N_B, HEAD_DIM), 0.1),
        'b_lam_k1': nrm((N_B, HEAD_DIM), 0.1),
        'b_lam_q2': nrm((N_B, HEAD_DIM), 0.1),
        'b_lam_k2': nrm((N_B, HEAD_DIM), 0.1),
        'b_subln_g': gain((N_B, 2 * HEAD_DIM)),
        'b_w_o': nrm((N_B, d, d), dsc),
        'c_w_qkv': nrm((N_C, d, 3 * d), dsc),
        'c_w_qidx': nrm((N_C, d, N_IDX_HEADS * IDX_DIM), dsc),
        'c_w_kidx': nrm((N_C, d, IDX_DIM), dsc),
        'c_w_widx': nrm((N_C, d, N_IDX_HEADS), dsc),
        'c_w_o': nrm((N_C, d, d), dsc),
        'ffn_w_up': nrm((DEPTH, d, D_FF), dsc),
        'ffn_w_down': nrm((DEPTH, D_FF, d), D_FF ** -0.5),
    }
    return inputs


def reference(x_prompt, x_sample, cache_a_k, cache_a_v, cache_a_logf, cache_b_k, cache_b_v,
              cache_c_k, cache_c_v, cache_c_kidx, norm_mix_g, norm_ffn_g, norm_final_g,
              a_w_qkv, a_w_f, a_b_f, a_w_o, b_w_qkv, b_lam_q1, b_lam_k1, b_lam_q2, b_lam_k2,
              b_subln_g, b_w_o, c_w_qkv, c_w_qidx, c_w_kidx, c_w_widx, c_w_o, ffn_w_up, ffn_w_down):
    t_p = x_prompt.shape[1]
    t_s = x_sample.shape[1]
    past_len = cache_a_k.shape[2]
    pos_p = jnp.arange(t_p, dtype=jnp.int32)
    pos_s = past_len + jnp.arange(t_s, dtype=jnp.int32)
    past_pos = jnp.arange(past_len, dtype=jnp.int32)

    hp, hs = x_prompt, x_sample
    a_prompt, a_sample, b_prompt, b_sample, c_prompt, c_sample = [], [], [], [], [], []
    for i in range(DEPTH):
        kind = i % N_MIXERS
        j = i // N_MIXERS
        hp_n = rms_norm(hp, norm_mix_g[i])
        hs_n = rms_norm(hs, norm_mix_g[i])
        if kind == 0:
            wts = (a_w_qkv[j], a_w_f[j], a_b_f[j], a_w_o[j])
            yp, *sp = fox_mixer(hp_n, pos_p, None, *wts)
            ys, *ss = fox_mixer(hs_n, pos_s, (cache_a_k[j], cache_a_v[j], cache_a_logf[j], past_pos), *wts)
            a_prompt.append(sp)
            a_sample.append(ss)
        elif kind == 1:
            wts = (b_w_qkv[j], b_lam_q1[j], b_lam_k1[j], b_lam_q2[j], b_lam_k2[j], b_subln_g[j], b_w_o[j])
            yp, *sp = diff_mixer(hp_n, pos_p, None, i, *wts)
            ys, *ss = diff_mixer(hs_n, pos_s, (cache_b_k[j], cache_b_v[j], past_pos), i, *wts)
            b_prompt.append(sp)
            b_sample.append(ss)
        else:
            wts = (c_w_qkv[j], c_w_qidx[j], c_w_kidx[j], c_w_widx[j], c_w_o[j])
            yp, *sp = dsa_mixer(hp_n, pos_p, None, *wts)
            ys, *ss = dsa_mixer(hs_n, pos_s, (cache_c_k[j], cache_c_v[j], cache_c_kidx[j], past_pos), *wts)
            c_prompt.append(sp)
            c_sample.append(ss)
        hp = hp + yp
        hs = hs + ys
        hp = hp + sq_relu_ffn(rms_norm(hp, norm_ffn_g[i]), ffn_w_up[i], ffn_w_down[i])
        hs = hs + sq_relu_ffn(rms_norm(hs, norm_ffn_g[i]), ffn_w_up[i], ffn_w_down[i])

    y_prompt = rms_norm(hp, norm_final_g)
    y_sample = rms_norm(hs, norm_final_g)

    new_a_k_p = jnp.stack([s[0] for s in a_prompt])
    new_a_v_p = jnp.stack([s[1] for s in a_prompt])
    new_a_logf_p = jnp.stack([s[2] for s in a_prompt])
    new_b_k_p = jnp.stack([s[0] for s in b_prompt])
    new_b_v_p = jnp.stack([s[1] for s in b_prompt])
    new_c_k_p = jnp.stack([s[0] for s in c_prompt])
    new_c_v_p = jnp.stack([s[1] for s in c_prompt])
    new_c_kidx_p = jnp.stack([s[2] for s in c_prompt])
    new_a_k_s = jnp.stack([s[0] for s in a_sample])
    new_a_v_s = jnp.stack([s[1] for s in a_sample])
    new_a_logf_s = jnp.stack([s[2] for s in a_sample])
    new_b_k_s = jnp.stack([s[0] for s in b_sample])
    new_b_v_s = jnp.stack([s[1] for s in b_sample])
    new_c_k_s = jnp.stack([s[0] for s in c_sample])
    new_c_v_s = jnp.stack([s[1] for s in c_sample])
    new_c_kidx_s = jnp.stack([s[2] for s in c_sample])
    return (y_prompt, y_sample,
            new_a_k_p, new_a_v_p, new_a_logf_p, new_b_k_p, new_b_v_p, new_c_k_p, new_c_v_p, new_c_kidx_p,
            new_a_k_s, new_a_v_s, new_a_logf_s, new_b_k_s, new_b_v_s, new_c_k_s, new_c_v_s, new_c_kidx_s)
```

```python
import functools
import math

import jax
import jax.numpy as jnp
from jax import lax
from jax.experimental import pallas as pl
from jax.experimental.pallas import tpu as pltpu

F32 = jnp.float32
BF16 = jnp.bfloat16
I32 = jnp.int32

D_MODEL = 1024
HEAD_DIM = 64
CHUNK_SHIFT = 6
N_IDX_HEADS = 8
IDX_DIM = 64
TOPK_MAX = 256
N_MIXERS = 3
ROPE_THETA = 10000.0
NORM_EPS = 1e-6
SUBLN_EPS = 1e-5
ATTN_SCALE = HEAD_DIM ** -0.5
IDX_SCALE = IDX_DIM ** -0.5
IDX_HEAD_SCALE = N_IDX_HEADS ** -0.5

LANES = 128
NEG = -1e30
INT_MIN = -(2 ** 31)
VMEM_LIMIT_BYTES = 48 * 1024 * 1024


def _cparams(*sem):
    return pltpu.CompilerParams(dimension_semantics=sem, vmem_limit_bytes=VMEM_LIMIT_BYTES)


def _tile(n, pref):
    return pref if n % pref == 0 else n


def _rms(x, g, eps):
    ms = jnp.mean(x * x, axis=-1, keepdims=True)
    return x * lax.rsqrt(ms + eps) * g


def _rope_lanes(y, cos, sin_signed, first_half):
    partner = jnp.where(first_half, pltpu.roll(y, 96, 1), pltpu.roll(y, 32, 1))
    return y * cos + partner * sin_signed


def _rope_tables(pos):
    half = HEAD_DIM // 2
    inv = ROPE_THETA ** (-jnp.arange(half, dtype=F32) / half)
    ang = pos.astype(F32)[:, None] * inv[None, :]
    cos, sin = jnp.cos(ang), jnp.sin(ang)
    cos_h = jnp.concatenate([cos, cos], axis=-1)
    sin_h = jnp.concatenate([-sin, sin], axis=-1)
    return jnp.tile(cos_h, (1, 2)), jnp.tile(sin_h, (1, 2))


def _proj_kernel(specs, tn, x_ref, g_ref, w_ref, cos_ref, sin_ref, *rest):
    out_refs, xn_ref = rest[:len(specs)], rest[len(specs)]
    j = pl.program_id(1)

    @pl.when(j == 0)
    def _():
        xn_ref[...] = _rms(x_ref[...], g_ref[...], NORM_EPS).astype(BF16)

    acc = jnp.dot(xn_ref[...], w_ref[...], preferred_element_type=F32)
    for (lo, hi, rope, scale), o_ref in zip(specs, out_refs):

        @pl.when((j >= lo) & (j < hi))
        def _(o_ref=o_ref, rope=rope, scale=scale):
            if rope:
                cos, sin = cos_ref[...], sin_ref[...]
                first = (lax.broadcasted_iota(I32, cos.shape, 1) % HEAD_DIM) < HEAD_DIM // 2
                for c in range(tn // LANES):
                    sl = slice(c * LANES, (c + 1) * LANES)
                    y = _rope_lanes(acc[:, sl], cos, sin, first)
                    o_ref[:, sl] = (y * scale).astype(o_ref.dtype)
            else:
                o_ref[...] = (acc * scale).astype(o_ref.dtype)


def _proj(x, g, w, outs, cos, sin, tm_pref=512, tn=512):
    n, d = x.shape
    tm = _tile(n, tm_pref)
    t_tab = cos.shape[0]
    n_tab = t_tab // tm
    specs, out_shapes, out_specs, lo = [], [], [], 0
    for cols, rope, scale, dtype in outs:
        nt = cols // tn
        specs.append((lo, lo + nt, rope, scale))
        out_shapes.append(jax.ShapeDtypeStruct((n, cols), dtype))
        out_specs.append(pl.BlockSpec((tm, tn), functools.partial(
            lambda i, j, lo, nt: (i, jnp.clip(j - lo, 0, nt - 1)), lo=lo, nt=nt)))
        lo += nt
    assert lo * tn == w.shape[1]
    return pl.pallas_call(
        functools.partial(_proj_kernel, tuple(specs), tn),
        grid=(n // tm, lo),
        in_specs=[
            pl.BlockSpec((tm, d), lambda i, j: (i, 0)),
            pl.BlockSpec((1, d), lambda i, j: (0, 0)),
            pl.BlockSpec((d, tn), lambda i, j: (0, j)),
            pl.BlockSpec((tm, LANES), lambda i, j: (i % n_tab, 0)),
            pl.BlockSpec((tm, LANES), lambda i, j: (i % n_tab, 0)),
        ],
        out_specs=out_specs,
        out_shape=out_shapes,
        scratch_shapes=[pltpu.VMEM((tm, d), BF16)],
        compiler_params=_cparams("parallel", "arbitrary"),
        name="proj",
    )(x, g.reshape(1, d), w, cos, sin)


def _small_proj_kernel(kind, x_ref, g_ref, w_ref, b_ref, cos_ref, sin_ref, o_ref):
    xn = _rms(x_ref[...], g_ref[...], NORM_EPS)
    z = jnp.dot(xn, w_ref[...], preferred_element_type=F32, precision=lax.Precision.HIGHEST)
    if kind == "fgate":
        z = z + b_ref[...]
        o_ref[...] = jnp.minimum(z, 0.0) - jnp.log1p(jnp.exp(-jnp.abs(z)))
    else:
        lane = lax.broadcasted_iota(I32, z.shape, 1)
        first = (lane % HEAD_DIM) < HEAD_DIM // 2
        roped = _rope_lanes(z, cos_ref[...], sin_ref[...], first)
        o_ref[...] = jnp.where(lane < IDX_DIM, roped, z * IDX_HEAD_SCALE)


def _small_proj(kind, x, g, w, b, cos, sin, tm_pref=512):
    n, d = x.shape
    tm = _tile(n, tm_pref)
    n_tab = cos.shape[0] // tm
    return pl.pallas_call(
        functools.partial(_small_proj_kernel, kind),
        grid=(n // tm,),
        in_specs=[
            pl.BlockSpec((tm, d), lambda i: (i, 0)),
            pl.BlockSpec((1, d), lambda i: (0, 0)),
            pl.BlockSpec((d, LANES), lambda i: (0, 0)),
            pl.BlockSpec((1, LANES), lambda i: (0, 0)),
            pl.BlockSpec((tm, LANES), lambda i: (i % n_tab, 0)),
            pl.BlockSpec((tm, LANES), lambda i: (i % n_tab, 0)),
        ],
        out_specs=pl.BlockSpec((tm, LANES), lambda i: (i, 0)),
        out_shape=jax.ShapeDtypeStruct((n, LANES), F32),
        compiler_params=_cparams("parallel"),
        name="small_proj_" + kind,
    )(x, g.reshape(1, d), w, b, cos, sin)


def _wo_kernel(a_ref, w_ref, r_ref, o_ref):
    o_ref[...] = r_ref[...] + jnp.dot(a_ref[...], w_ref[...], preferred_element_type=F32)


def _wo(a, w, res, tm_pref=512):
    n, d = res.shape
    tm = _tile(n, tm_pref)
    return pl.pallas_call(
        _wo_kernel,
        grid=(n // tm,),
        in_specs=[
            pl.BlockSpec((tm, d), lambda i: (i, 0)),
            pl.BlockSpec((d, d), lambda i: (0, 0)),
            pl.BlockSpec((tm, d), lambda i: (i, 0)),
        ],
        out_specs=pl.BlockSpec((tm, d), lambda i: (i, 0)),
        out_shape=jax.ShapeDtypeStruct((n, d), F32),
        compiler_params=_cparams("parallel"),
        name="wo",
    )(a, w, res)


def _ffn_kernel(nf, x_ref, g_ref, wu_ref, wd_ref, o_ref, xn_ref, acc_ref):
    f = pl.program_id(1)

    @pl.when(f == 0)
    def _():
        xn_ref[...] = _rms(x_ref[...], g_ref[...], NORM_EPS).astype(BF16)
        acc_ref[...] = jnp.zeros_like(acc_ref)

    u = jnp.dot(xn_ref[...], wu_ref[...], preferred_element_type=F32)
    u = jnp.maximum(u, 0.0)
    acc_ref[...] += jnp.dot((u * u).astype(BF16), wd_ref[...], preferred_element_type=F32)

    @pl.when(f == nf - 1)
    def _():
        o_ref[...] = x_ref[...] + acc_ref[...]


def _ffn(x, g, w_up, w_down, tm_pref=1024, tf=512):
    n, d = x.shape
    dff = w_up.shape[1]
    tm = _tile(n, tm_pref)
    nf = dff // tf
    return pl.pallas_call(
        functools.partial(_ffn_kernel, nf),
        grid=(n // tm, nf),
        in_specs=[
            pl.BlockSpec((tm, d), lambda i, f: (i, 0)),
            pl.BlockSpec((1, d), lambda i, f: (0, 0)),
            pl.BlockSpec((d, tf), lambda i, f: (0, f)),
            pl.BlockSpec((tf, d), lambda i, f: (f, 0)),
        ],
        out_specs=pl.BlockSpec((tm, d), lambda i, f: (i, 0)),
        out_shape=jax.ShapeDtypeStruct((n, d), F32),
        scratch_shapes=[pltpu.VMEM((tm, d), BF16), pltpu.VMEM((tm, d), F32)],
        compiler_params=_cparams("parallel", "arbitrary"),
        name="ffn",
    )(x, g.reshape(1, d), w_up, w_down)


def _final_norm_kernel(x_ref, g_ref, o_ref):
    o_ref[...] = _rms(x_ref[...], g_ref[...], NORM_EPS)


def _final_norm(x, g, tm_pref=512):
    n, d = x.shape
    tm = _tile(n, tm_pref)
    return pl.pallas_call(
        _final_norm_kernel,
        grid=(n // tm,),
        in_specs=[pl.BlockSpec((tm, d), lambda i: (i, 0)), pl.BlockSpec((1, d), lambda i: (0, 0))],
        out_specs=pl.BlockSpec((tm, d), lambda i: (i, 0)),
        out_shape=jax.ShapeDtypeStruct((n, d), F32),
        compiler_params=_cparams("parallel"),
        name="final_norm",
    )(x, g.reshape(1, d))


def _cumsum_kernel(x_ref, o_ref):
    x = x_ref[0]
    lane = lax.broadcasted_iota(I32, x.shape, 1)
    d = 1
    while d < x.shape[1]:
        x = x + jnp.where(lane >= d, pltpu.roll(x, d, 1), 0.0)
        d *= 2
    o_ref[0] = x


def _cumsum_time(x):
    b, h, t = x.shape
    return pl.pallas_call(
        _cumsum_kernel,
        grid=(b,),
        in_specs=[pl.BlockSpec((1, h, t), lambda i: (i, 0, 0))],
        out_specs=pl.BlockSpec((1, h, t), lambda i: (i, 0, 0)),
        out_shape=jax.ShapeDtypeStruct((b, h, t), F32),
        compiler_params=_cparams("parallel"),
        name="cumsum_time",
    )(x)


def _qk(q, k):
    return lax.dot_general(q, k, (((1,), (1,)), ((), ())), preferred_element_type=F32)


def _online_update(s, v, m_ref, l_ref, acc_ref):
    m_prev = m_ref[...]
    m_new = jnp.maximum(m_prev, jnp.max(s, axis=1, keepdims=True))
    alpha = jnp.exp(m_prev - m_new)
    p = jnp.exp(s - m_new)
    l_ref[...] = alpha * l_ref[...] + jnp.sum(p, axis=1, keepdims=True)
    acc_ref[...] = alpha * acc_ref[...] + jnp.dot(p.astype(BF16), v, preferred_element_type=F32)
    m_ref[...] = m_new


def _attn_init(m_ref, l_ref, acc_ref):
    m_ref[...] = jnp.full(m_ref.shape, NEG, F32)
    l_ref[...] = jnp.zeros(l_ref.shape, F32)
    acc_ref[...] = jnp.zeros(acc_ref.shape, F32)


def _head_column(cq_blk, h):
    lane = lax.broadcasted_iota(I32, cq_blk.shape, 1)
    return jnp.sum(jnp.where(lane == h, cq_blk, 0.0), axis=1, keepdims=True)


def _attn_finalize(mode, lam_init, lam_ref, g_ref, o_ref, l_ref, acc_ref):
    o0 = acc_ref[0] / l_ref[0]
    o1 = acc_ref[1] / l_ref[1]
    if mode == "diff":
        lp = lam_ref[...]
        lam = (jnp.exp(jnp.sum(lp[0:1] * lp[1:2], axis=1, keepdims=True))
               - jnp.exp(jnp.sum(lp[2:3] * lp[3:4], axis=1, keepdims=True)) + lam_init)
        o = o0 - lam * o1
        o = _rms(o, g_ref[...], SUBLN_EPS) * (1.0 - lam_init)
    else:
        o = jnp.concatenate([o0, o1], axis=1)
    o_ref[0] = o.astype(o_ref.dtype)


def _attn_block(mode, hp, q2, k2, v2, keep, cq_blk, ck_ref, bias, m_ref, l_ref, acc_ref):
    for j in range(2):
        sl = slice(j * HEAD_DIM, (j + 1) * HEAD_DIM)
        s = _qk(q2[:, sl], k2[:, sl])
        if mode == "fox":
            h = 2 * hp + j
            s = s + _head_column(cq_blk, h) - ck_ref[0, pl.ds(h, 1), :][:, :s.shape[1]]
        if bias is not None:
            s = s + bias
        if keep is not None:
            s = jnp.where(keep, s, NEG)
        v = v2 if mode == "diff" else v2[:, sl]
        _online_update(s, v, m_ref.at[j], l_ref.at[j], acc_ref.at[j])


def _attn_prompt_kernel(mode, tq, tk, nk, lam_init, *refs):
    q_ref, k_ref, v_ref = refs[:3]
    cq_ref = ck_ref = bias_ref = lam_ref = g_ref = None
    if mode == "fox":
        cq_ref, ck_ref = refs[3:5]
        rest = refs[5:]
    elif mode == "diff":
        lam_ref, g_ref = refs[3:5]
        rest = refs[5:]
    else:
        bias_ref = refs[3]
        rest = refs[4:]
    o_ref, m_ref, l_ref, acc_ref = rest
    hp, qi, ki = pl.program_id(1), pl.program_id(2), pl.program_id(3)

    @pl.when(ki == 0)
    def _():
        _attn_init(m_ref, l_ref, acc_ref)

    q_lo, k_lo = qi * tq, ki * tk
    needed = k_lo <= q_lo + tq - 1
    straddle = k_lo + tk - 1 > q_lo

    def process(masked):
        keep = None
        if masked:
            rows = q_lo + lax.broadcasted_iota(I32, (tq, tk), 0)
            cols = k_lo + lax.broadcasted_iota(I32, (tq, tk), 1)
            keep = cols <= rows if mode == "fox" else (cols >> CHUNK_SHIFT) <= (rows >> CHUNK_SHIFT)
        bias = bias_ref[0, 0].astype(F32) if mode == "dsa" else None
        cq_blk = cq_ref[0] if mode == "fox" else None
        _attn_block(mode, hp, q_ref[0], k_ref[0].astype(BF16), v_ref[0].astype(BF16),
                    keep, cq_blk, ck_ref, bias, m_ref, l_ref, acc_ref)

    if mode == "dsa":
        pl.when(needed)(lambda: process(False))
    else:
        pl.when(needed & straddle)(lambda: process(True))
        pl.when(needed & jnp.logical_not(straddle))(lambda: process(False))

    @pl.when(ki == nk - 1)
    def _():
        _attn_finalize(mode, lam_init, lam_ref, g_ref, o_ref, l_ref, acc_ref)


def _attn_prompt(mode, q, k, v, extras, lam_init=0.0, tq_pref=512, tk_pref=512):
    b, t, d = q.shape
    tq, tk = _tile(t, tq_pref), _tile(t, tk_pref)
    nq, nk = t // tq, t // tk
    npair = d // LANES
    dv = LANES if mode == "diff" else HEAD_DIM

    def kmax(qi):
        return (qi * tq + tq - 1) // tk

    in_specs = [
        pl.BlockSpec((1, tq, LANES), lambda bi, hp, qi, ki: (bi, qi, hp)),
        pl.BlockSpec((1, tk, LANES), lambda bi, hp, qi, ki: (bi, jnp.minimum(ki, kmax(qi)), hp)),
        pl.BlockSpec((1, tk, LANES), lambda bi, hp, qi, ki: (bi, jnp.minimum(ki, kmax(qi)), hp)),
    ]
    if mode == "fox":
        nh = extras[0].shape[2]
        in_specs += [
            pl.BlockSpec((1, tq, nh), lambda bi, hp, qi, ki: (bi, qi, 0)),
            pl.BlockSpec((1, nh, tk), lambda bi, hp, qi, ki: (bi, 0, jnp.minimum(ki, kmax(qi)))),
        ]
    elif mode == "diff":
        in_specs += [
            pl.BlockSpec((4, HEAD_DIM), lambda bi, hp, qi, ki: (0, 0)),
            pl.BlockSpec((1, LANES), lambda bi, hp, qi, ki: (0, 0)),
        ]
    else:
        in_specs += [
            pl.BlockSpec((1, 1, tq, tk), lambda bi, hp, qi, ki: (bi, jnp.minimum(ki, kmax(qi)), qi, 0)),
        ]
    return pl.pallas_call(
        functools.partial(_attn_prompt_kernel, mode, tq, tk, nk, lam_init),
        grid=(b, npair, nq, nk),
        in_specs=in_specs,
        out_specs=pl.BlockSpec((1, tq, LANES), lambda bi, hp, qi, ki: (bi, qi, hp)),
        out_shape=jax.ShapeDtypeStruct((b, t, d), BF16),
        scratch_shapes=[pltpu.VMEM((2, tq, 1), F32), pltpu.VMEM((2, tq, 1), F32), pltpu.VMEM((2, tq, dv), F32)],
        compiler_params=_cparams("parallel", "parallel", "parallel", "arbitrary"),
        name="attn_prompt_" + mode,
    )(q, k, v, *extras)


def _attn_sample_kernel(mode, ts, tk, nkp, lam_init, *refs):
    q_ref, pk_ref, pv_ref, nk_ref, nv_ref = refs[:5]
    cq_ref = ckp_ref = ckn_ref = bp_ref = bn_ref = lam_ref = g_ref = None
    if mode == "fox":
        cq_ref, ckp_ref, ckn_ref = refs[5:8]
        rest = refs[8:]
    elif mode == "diff":
        lam_ref, g_ref = refs[5:7]
        rest = refs[7:]
    else:
        bp_ref, bn_ref = refs[5:7]
        rest = refs[7:]
    o_ref, m_ref, l_ref, acc_ref = rest
    hp, ki = pl.program_id(1), pl.program_id(2)

    @pl.when(ki == 0)
    def _():
        _attn_init(m_ref, l_ref, acc_ref)

    cq_blk = cq_ref[0] if mode == "fox" else None
    bias = bp_ref[0, 0].astype(F32) if mode == "dsa" else None
    _attn_block(mode, hp, q_ref[0], pk_ref[0, 0].astype(BF16), pv_ref[0, 0].astype(BF16),
                None, cq_blk, ckp_ref, bias, m_ref, l_ref, acc_ref)

    @pl.when(ki == nkp - 1)
    def _():
        keep = None
        if mode == "fox":
            keep = lax.broadcasted_iota(I32, (ts, ts), 1) <= lax.broadcasted_iota(I32, (ts, ts), 0)
        bias_n = bn_ref[0, 0][:, :ts].astype(F32) if mode == "dsa" else None
        _attn_block(mode, hp, q_ref[0], nk_ref[0].astype(BF16), nv_ref[0].astype(BF16),
                    keep, cq_blk, ckn_ref, bias_n, m_ref, l_ref, acc_ref)
        _attn_finalize(mode, lam_init, lam_ref, g_ref, o_ref, l_ref, acc_ref)


def _attn_sample(mode, layer, q, past_k, past_v, new_k, new_v, extras, lam_init=0.0, tk_pref=512):
    b, ts, d = q.shape
    p = past_k.shape[2]
    tk = _tile(p, tk_pref)
    nkp = p // tk
    npair = d // LANES
    dv = LANES if mode == "diff" else HEAD_DIM
    in_specs = [
        pl.BlockSpec((1, ts, LANES), lambda bi, hp, ki: (bi, 0, hp)),
        pl.BlockSpec((1, 1, tk, LANES), lambda bi, hp, ki: (layer, bi, ki, hp)),
        pl.BlockSpec((1, 1, tk, LANES), lambda bi, hp, ki: (layer, bi, ki, hp)),
        pl.BlockSpec((1, ts, LANES), lambda bi, hp, ki: (bi, 0, hp)),
        pl.BlockSpec((1, ts, LANES), lambda bi, hp, ki: (bi, 0, hp)),
    ]
    if mode == "fox":
        nh = extras[0].shape[2]
        in_specs += [
            pl.BlockSpec((1, ts, nh), lambda bi, hp, ki: (bi, 0, 0)),
            pl.BlockSpec((1, nh, tk), lambda bi, hp, ki: (bi, 0, ki)),
            pl.BlockSpec((1, nh, LANES), lambda bi, hp, ki: (bi, 0, p // LANES)),
        ]
    elif mode == "diff":
        in_specs += [
            pl.BlockSpec((4, HEAD_DIM), lambda bi, hp, ki: (0, 0)),
            pl.BlockSpec((1, LANES), lambda bi, hp, ki: (0, 0)),
        ]
    else:
        in_specs += [
            pl.BlockSpec((1, 1, ts, tk), lambda bi, hp, ki: (bi, ki, 0, 0)),
            pl.BlockSpec((1, 1, ts, tk), lambda bi, hp, ki: (bi, nkp, 0, 0)),
        ]
    return pl.pallas_call(
        functools.partial(_attn_sample_kernel, mode, ts, tk, nkp, lam_init),
        grid=(b, npair, nkp),
        in_specs=in_specs,
        out_specs=pl.BlockSpec((1, ts, LANES), lambda bi, hp, ki: (bi, 0, hp)),
        out_shape=jax.ShapeDtypeStruct((b, ts, d), BF16),
        scratch_shapes=[pltpu.VMEM((2, ts, 1), F32), pltpu.VMEM((2, ts, 1), F32), pltpu.VMEM((2, ts, dv), F32)],
        compiler_params=_cparams("parallel", "parallel", "arbitrary"),
        name="attn_sample_" + mode,
    )(q, past_k, past_v, new_k, new_v, *extras)


def _dsa_select_kernel(tq, tk, nkb, top_k, q_pos0, n_valid, causal, qi_ref, w_ref, kk_ref, tri_ref, bias_ref, key_ref):
    i = pl.program_id(1)
    nb = (i * tq + tq - 1) // tk + 1 if causal else nkb
    q_chunk = (q_pos0 + i * tq + lax.broadcasted_iota(I32, (tq, tk), 0)) >> CHUNK_SHIFT
    col = lax.broadcasted_iota(I32, (tq, tk), 1)

    def score_block(kb, carry):
        kblk = kk_ref[0, pl.ds(pl.multiple_of(kb * tk, tk), tk), :][:, :IDX_DIM].astype(BF16)
        sc = jnp.zeros((tq, tk), F32)
        for h in range(N_IDX_HEADS):
            s = _qk(qi_ref[0, :, h * IDX_DIM:(h + 1) * IDX_DIM], kblk)
            sc = sc + jnp.maximum(s, 0.0) * w_ref[0, :, IDX_DIM + h:IDX_DIM + h + 1]
        sc = jnp.where(sc == 0.0, 0.0, sc)
        kidx = kb * tk + col
        adm = ((kidx >> CHUNK_SHIFT) <= q_chunk) & (kidx < n_valid)
        bits = pltpu.bitcast(sc, I32)
        key = jnp.where(bits >= 0, bits, bits ^ jnp.int32(0x7FFFFFFF))
        key_ref[kb] = jnp.where(adm, key, jnp.int32(INT_MIN))
        return carry

    lax.fori_loop(0, nb, score_block, 0)

    def count(pred):
        def body(kb, acc):
            key = key_ref[kb]
            for c in range(tk // LANES):
                acc = acc + jnp.where(pred(key[:, c * LANES:(c + 1) * LANES]), 1.0, 0.0)
            return acc
        acc = lax.fori_loop(0, nb, body, jnp.zeros((tq, LANES), F32))
        return jnp.sum(acc, axis=1, keepdims=True)

    def bit_body(it, u):
        cand = u | lax.shift_left(jnp.int32(1), 31 - it)
        cand_b = jnp.broadcast_to(cand ^ jnp.int32(INT_MIN), (tq, LANES))
        cnt = count(lambda key: key >= cand_b)
        return jnp.where(cnt >= top_k, cand, u)

    u = lax.fori_loop(0, 32, bit_body, jnp.zeros((tq, 1), I32))
    thr = u ^ jnp.int32(INT_MIN)
    thr_b = jnp.broadcast_to(thr, (tq, LANES))
    need = top_k - count(lambda key: key > thr_b)

    def out_body(kb, seen):
        key = key_ref[kb]
        eq = (key == thr) & (key > jnp.int32(INT_MIN))
        eqf = jnp.where(eq, 1.0, 0.0)
        rank = seen + jnp.dot(eqf.astype(BF16), tri_ref[...], preferred_element_type=F32)
        sel = (key > thr) | (eq & (rank < need))
        bias_ref[0, kb] = jnp.where(sel, 0.0, NEG).astype(bias_ref.dtype)
        return seen + jnp.sum(eqf, axis=1, keepdims=True)

    lax.fori_loop(0, nb, out_body, jnp.zeros((tq, 1), F32))

    def fill(kb, carry):
        bias_ref[0, kb] = jnp.full((tq, tk), NEG, bias_ref.dtype)
        return carry

    lax.fori_loop(nb, nkb, fill, 0)


def _dsa_select(qidx, wsrc, keys, q_pos0, n_valid, causal, tq, tk):
    b, t, _ = qidx.shape
    s, kd = keys.shape[1], keys.shape[2]
    nkb = s // tk
    top_k = min(TOPK_MAX, n_valid // 4)
    tri = (jnp.arange(tk)[:, None] < jnp.arange(tk)[None, :]).astype(BF16)
    return pl.pallas_call(
        functools.partial(_dsa_select_kernel, tq, tk, nkb, top_k, q_pos0, n_valid, causal),
        grid=(b, t // tq),
        in_specs=[
            pl.BlockSpec((1, tq, N_IDX_HEADS * IDX_DIM), lambda bi, i: (bi, i, 0)),
            pl.BlockSpec((1, tq, LANES), lambda bi, i: (bi, i, 0)),
            pl.BlockSpec((1, s, kd), lambda bi, i: (bi, 0, 0)),
            pl.BlockSpec((tk, tk), lambda bi, i: (0, 0)),
        ],
        out_specs=pl.BlockSpec((1, nkb, tq, tk), lambda bi, i: (bi, 0, i, 0)),
        out_shape=jax.ShapeDtypeStruct((b, nkb, t, tk), BF16),
        scratch_shapes=[pltpu.VMEM((nkb, tq, tk), I32)],
        compiler_params=_cparams("parallel", "arbitrary"),
        name="dsa_select",
    )(qidx, wsrc, keys, tri)


def _pad_cols(w, n):
    return jnp.pad(w, ((0, 0), (0, n - w.shape[1])))


def _diff_lambda_init(layer):
    return 0.8 - 0.6 * math.exp(-0.3 * layer)


def _fox_layer(j, g, hp, hs, bp, bs, cache_k, cache_v, cache_logf, w_qkv, w_f, b_f, w_o, tabs_p, tabs_s):
    d = D_MODEL
    nh = w_f.shape[1]
    w = w_qkv.astype(BF16)
    outs = [(d, False, ATTN_SCALE, BF16), (d, False, 1.0, F32), (d, False, 1.0, F32)]
    wf = _pad_cols(w_f, LANES)
    bf = _pad_cols(b_f.reshape(1, nh), LANES)
    res = []
    for h, nb, tabs in ((hp, bp, tabs_p), (hs, bs, tabs_s)):
        q, k, v = _proj(h, g, w, outs, *tabs)
        logf = _small_proj("fgate", h, g, wf, bf, *tabs)[:, :nh]
        t = h.shape[0] // nb
        res.append((q.reshape(nb, t, d), k.reshape(nb, t, d), v.reshape(nb, t, d), logf.reshape(nb, t, nh)))
    (qp, kp, vp, lfp), (qs, ks, vs, lfs) = res

    cum_row = _cumsum_time(jnp.swapaxes(lfp, 1, 2))
    op = _attn_prompt("fox", qp, kp, vp, (jnp.swapaxes(cum_row, 1, 2), cum_row))

    past = cache_logf.shape[2]
    ts = lfs.shape[1]
    lf_all = jnp.swapaxes(jnp.concatenate([cache_logf[j], lfs], axis=1), 1, 2)
    padded = -(-(past + ts) // LANES) * LANES
    cum_s = _cumsum_time(jnp.pad(lf_all, ((0, 0), (0, 0), (0, padded - past - ts))))
    cq_s = jnp.swapaxes(cum_s[:, :, past:past + ts], 1, 2)
    ck = cache_k.reshape(*cache_k.shape[:3], d)
    cv = cache_v.reshape(*cache_v.shape[:3], d)
    os_ = _attn_sample("fox", j, qs, ck, cv, ks, vs, (cq_s, cum_s, cum_s))

    wo = w_o.astype(BF16)
    hp = _wo(op.reshape(-1, d), wo, hp)
    hs = _wo(os_.reshape(-1, d), wo, hs)
    return hp, hs, (kp, vp, lfp), (ks, vs, lfs)


def _diff_layer(layer, j, g, hp, hs, bp, bs, cache_k, cache_v, w_qkv, lams, subln_g, w_o, tabs_p, tabs_s):
    d = D_MODEL
    w = w_qkv.astype(BF16)
    outs = [(d, True, ATTN_SCALE, BF16), (d, True, 1.0, F32), (d, False, 1.0, F32)]
    lam_init = _diff_lambda_init(layer)
    extras = (lams, subln_g.reshape(1, LANES))
    qp, kp, vp = (a.reshape(bp, -1, d) for a in _proj(hp, g, w, outs, *tabs_p))
    qs, ks, vs = (a.reshape(bs, -1, d) for a in _proj(hs, g, w, outs, *tabs_s))
    op = _attn_prompt("diff", qp, kp, vp, extras, lam_init)
    ck = cache_k.reshape(*cache_k.shape[:3], d)
    cv = cache_v.reshape(*cache_v.shape[:3], d)
    os_ = _attn_sample("diff", j, qs, ck, cv, ks, vs, extras, lam_init)
    wo = w_o.astype(BF16)
    hp = _wo(op.reshape(-1, d), wo, hp)
    hs = _wo(os_.reshape(-1, d), wo, hs)
    return hp, hs, (kp, vp), (ks, vs)


def _dsa_layer(j, g, hp, hs, bp, bs, cache_k, cache_v, cache_kidx, w_qkv, w_qidx, w_kidx, w_widx, w_o, tabs_p, tabs_s):
    d = D_MODEL
    ni = N_IDX_HEADS * IDX_DIM
    w = jnp.concatenate([w_qkv, w_qidx], axis=1).astype(BF16)
    outs = [(d, True, ATTN_SCALE, BF16), (d, True, 1.0, F32), (d, False, 1.0, F32), (ni, True, IDX_SCALE, BF16)]
    w_small = _pad_cols(jnp.concatenate([w_kidx, w_widx], axis=1), LANES)
    zero_b = jnp.zeros((1, LANES), F32)

    qp, kp, vp, qip = (a.reshape(bp, -1, a.shape[-1]) for a in _proj(hp, g, w, outs, *tabs_p))
    spp = _small_proj("dsa", hp, g, w_small, zero_b, *tabs_p).reshape(bp, -1, LANES)
    t = qp.shape[1]
    sel_tk = _tile(t, 512)
    bias_p = _dsa_select(qip, spp, spp, 0, t, True, _tile(t, 128), sel_tk)
    op = _attn_prompt("dsa", qp, kp, vp, (bias_p,), tq_pref=sel_tk, tk_pref=sel_tk)

    qs, ks, vs, qis = (a.reshape(bs, -1, a.shape[-1]) for a in _proj(hs, g, w, outs, *tabs_s))
    sps = _small_proj("dsa", hs, g, w_small, zero_b, *tabs_s).reshape(bs, -1, LANES)
    ts = qs.shape[1]
    past = cache_kidx.shape[2]
    tk_s = _tile(past, 512)
    keys_s = jnp.concatenate([cache_kidx[j], sps[:, :, :IDX_DIM]], axis=1)
    keys_s = jnp.pad(keys_s, ((0, 0), (0, tk_s - ts), (0, 0)))
    bias_s = _dsa_select(qis, sps, keys_s, past, past + ts, False, ts, tk_s)
    ck = cache_k.reshape(*cache_k.shape[:3], d)
    cv = cache_v.reshape(*cache_v.shape[:3], d)
    os_ = _attn_sample("dsa", j, qs, ck, cv, ks, vs, (bias_s, bias_s), tk_pref=tk_s)

    wo = w_o.astype(BF16)
    hp = _wo(op.reshape(-1, d), wo, hp)
    hs = _wo(os_.reshape(-1, d), wo, hs)
    return hp, hs, (kp, vp, spp[:, :, :IDX_DIM]), (ks, vs, sps[:, :, :IDX_DIM])


def kernel(x_prompt, x_sample, cache_a_k, cache_a_v, cache_a_logf, cache_b_k, cache_b_v, cache_c_k, cache_c_v, cache_c_kidx, norm_mix_g, norm_ffn_g, norm_final_g, a_w_qkv, a_w_f, a_b_f, a_w_o, b_w_qkv, b_lam_q1, b_lam_k1, b_lam_q2, b_lam_k2, b_subln_g, b_w_o, c_w_qkv, c_w_qidx, c_w_kidx, c_w_widx, c_w_o, ffn_w_up, ffn_w_down):
    bp, tp, d = x_prompt.shape
    bs, ts, _ = x_sample.shape
    past = cache_a_k.shape[2]
    depth = norm_mix_g.shape[0]
    tabs_p = _rope_tables(jnp.arange(tp, dtype=I32))
    cos_s, sin_s = _rope_tables(past + jnp.arange(ts, dtype=I32))
    tabs_s = (jnp.tile(cos_s, (bs, 1)), jnp.tile(sin_s, (bs, 1)))

    hp = x_prompt.reshape(bp * tp, d)
    hs = x_sample.reshape(bs * ts, d)
    new = {"a": ([], []), "b": ([], []), "c": ([], [])}
    for i in range(depth):
        kind, j = i % N_MIXERS, i // N_MIXERS
        g = norm_mix_g[i]
        if kind == 0:
            hp, hs, sp, ss = _fox_layer(j, g, hp, hs, bp, bs, cache_a_k, cache_a_v, cache_a_logf,
                                        a_w_qkv[j], a_w_f[j], a_b_f[j], a_w_o[j], tabs_p, tabs_s)
            key = "a"
        elif kind == 1:
            lams = jnp.stack([b_lam_q1[j], b_lam_k1[j], b_lam_q2[j], b_lam_k2[j]])
            hp, hs, sp, ss = _diff_layer(i, j, g, hp, hs, bp, bs, cache_b_k, cache_b_v,
                                         b_w_qkv[j], lams, b_subln_g[j], b_w_o[j], tabs_p, tabs_s)
            key = "b"
        else:
            hp, hs, sp, ss = _dsa_layer(j, g, hp, hs, bp, bs, cache_c_k, cache_c_v, cache_c_kidx,
                                        c_w_qkv[j], c_w_qidx[j], c_w_kidx[j], c_w_widx[j], c_w_o[j], tabs_p, tabs_s)
            key = "c"
        new[key][0].append(sp)
        new[key][1].append(ss)
        wu, wd = ffn_w_up[i].astype(BF16), ffn_w_down[i].astype(BF16)
        hp = _ffn(hp, norm_ffn_g[i], wu, wd)
        hs = _ffn(hs, norm_ffn_g[i], wu, wd)

    y_prompt = _final_norm(hp, norm_final_g).reshape(bp, tp, d)
    y_sample = _final_norm(hs, norm_final_g).reshape(bs, ts, d)

    def stack(items, idx, shape_tail):
        return jnp.stack([s[idx].reshape(*s[idx].shape[:2], *shape_tail) for s in items])

    nha = d // HEAD_DIM
    outs = [y_prompt, y_sample]
    for side in (0, 1):
        a, b, c = new["a"][side], new["b"][side], new["c"][side]
        outs += [
            stack(a, 0, (nha, HEAD_DIM)), stack(a, 1, (nha, HEAD_DIM)), stack(a, 2, (nha,)),
            stack(b, 0, (nha, HEAD_DIM)), stack(b, 1, (nha // 2, 2 * HEAD_DIM)),
            stack(c, 0, (nha, HEAD_DIM)), stack(c, 1, (nha, HEAD_DIM)), stack(c, 2, (IDX_DIM,)),
        ]
    return tuple(outs)
```

```python
import functools
import math

import jax
import jax.numpy as jnp
from jax import lax
from jax.experimental import pallas as pl
from jax.experimental.pallas import tpu as pltpu

F32 = jnp.float32
BF16 = jnp.bfloat16
I32 = jnp.int32

D_MODEL = 1024
HEAD_DIM = 64
CHUNK_SHIFT = 6
N_IDX_HEADS = 8
IDX_DIM = 64
TOPK_MAX = 256
N_MIXERS = 3
ROPE_THETA = 10000.0
NORM_EPS = 1e-6
SUBLN_EPS = 1e-5
ATTN_SCALE = HEAD_DIM ** -0.5
IDX_SCALE = IDX_DIM ** -0.5
IDX_HEAD_SCALE = N_IDX_HEADS ** -0.5

LANES = 128
NEG = -1e30
INT_MIN = -(2 ** 31)
VMEM_LIMIT_BYTES = 48 * 1024 * 1024
PROMPT_DEPTH = 1
SAMPLE_DEPTH = 16


def _cparams(*sem):
    return pltpu.CompilerParams(dimension_semantics=sem, vmem_limit_bytes=VMEM_LIMIT_BYTES)


def _tile(n, pref):
    return pref if n % pref == 0 else n


def _rms(x, g, eps):
    ms = jnp.mean(x * x, axis=-1, keepdims=True)
    return x * lax.rsqrt(ms + eps) * g


def _rope_lanes(y, cos, sin_signed, first_half):
    partner = jnp.where(first_half, pltpu.roll(y, 96, 1), pltpu.roll(y, 32, 1))
    return y * cos + partner * sin_signed


def _rope_tables(pos):
    half = HEAD_DIM // 2
    inv = ROPE_THETA ** (-jnp.arange(half, dtype=F32) / half)
    ang = pos.astype(F32)[:, None] * inv[None, :]
    cos, sin = jnp.cos(ang), jnp.sin(ang)
    cos_h = jnp.concatenate([cos, cos], axis=-1)
    sin_h = jnp.concatenate([-sin, sin], axis=-1)
    return jnp.tile(cos_h, (1, 2)), jnp.tile(sin_h, (1, 2))


def _proj_kernel(specs, tn, x_ref, g_ref, w_ref, cos_ref, sin_ref, *rest):
    out_refs, xn_ref = rest[:len(specs)], rest[len(specs)]
    j = pl.program_id(1)

    @pl.when(j == 0)
    def _():
        xn_ref[...] = _rms(x_ref[...], g_ref[...], NORM_EPS).astype(BF16)

    acc = jnp.dot(xn_ref[...], w_ref[...], preferred_element_type=F32)
    for (lo, hi, rope, scale), o_ref in zip(specs, out_refs):

        @pl.when((j >= lo) & (j < hi))
        def _(o_ref=o_ref, rope=rope, scale=scale):
            if rope:
                cos, sin = cos_ref[...], sin_ref[...]
                first = (lax.broadcasted_iota(I32, cos.shape, 1) % HEAD_DIM) < HEAD_DIM // 2
                for c in range(tn // LANES):
                    sl = slice(c * LANES, (c + 1) * LANES)
                    y = _rope_lanes(acc[:, sl], cos, sin, first)
                    o_ref[:, sl] = (y * scale).astype(o_ref.dtype)
            else:
                o_ref[...] = (acc * scale).astype(o_ref.dtype)


def _proj(x, g, w, outs, cos, sin, tm_pref=512, tn=512):
    n, d = x.shape
    tm = _tile(n, tm_pref)
    t_tab = cos.shape[0]
    n_tab = t_tab // tm
    specs, out_shapes, out_specs = [], [], []
    for start, cols, rope, scale, dtype in outs:
        lo, nt = start // tn, cols // tn
        specs.append((lo, lo + nt, rope, scale))
        out_shapes.append(jax.ShapeDtypeStruct((n, cols), dtype))
        out_specs.append(pl.BlockSpec((tm, tn), functools.partial(
            lambda i, j, lo, nt: (i, jnp.clip(j - lo, 0, nt - 1)), lo=lo, nt=nt)))
    return pl.pallas_call(
        functools.partial(_proj_kernel, tuple(specs), tn),
        grid=(n // tm, w.shape[1] // tn),
        in_specs=[
            pl.BlockSpec((tm, d), lambda i, j: (i, 0)),
            pl.BlockSpec((1, d), lambda i, j: (0, 0)),
            pl.BlockSpec((d, tn), lambda i, j: (0, j)),
            pl.BlockSpec((tm, LANES), lambda i, j: (i % n_tab, 0)),
            pl.BlockSpec((tm, LANES), lambda i, j: (i % n_tab, 0)),
        ],
        out_specs=out_specs,
        out_shape=out_shapes,
        scratch_shapes=[pltpu.VMEM((tm, d), BF16)],
        compiler_params=_cparams("parallel", "arbitrary"),
        name="proj",
    )(x, g.reshape(1, d), w, cos, sin)


def _small_proj_kernel(kind, x_ref, g_ref, w_ref, b_ref, cos_ref, sin_ref, o_ref):
    xn = _rms(x_ref[...], g_ref[...], NORM_EPS)
    z = jnp.dot(xn, w_ref[...], preferred_element_type=F32, precision=lax.Precision.HIGHEST)
    if kind == "fgate":
        z = z + b_ref[...]
        o_ref[...] = jnp.minimum(z, 0.0) - jnp.log1p(jnp.exp(-jnp.abs(z)))
    else:
        lane = lax.broadcasted_iota(I32, z.shape, 1)
        first = (lane % HEAD_DIM) < HEAD_DIM // 2
        roped = _rope_lanes(z, cos_ref[...], sin_ref[...], first)
        o_ref[...] = jnp.where(lane < IDX_DIM, roped, z * IDX_HEAD_SCALE)


def _small_proj(kind, x, g, w, b, cos, sin, tm_pref=512):
    n, d = x.shape
    tm = _tile(n, tm_pref)
    n_tab = cos.shape[0] // tm
    return pl.pallas_call(
        functools.partial(_small_proj_kernel, kind),
        grid=(n // tm,),
        in_specs=[
            pl.BlockSpec((tm, d), lambda i: (i, 0)),
            pl.BlockSpec((1, d), lambda i: (0, 0)),
            pl.BlockSpec((d, LANES), lambda i: (0, 0)),
            pl.BlockSpec((1, LANES), lambda i: (0, 0)),
            pl.BlockSpec((tm, LANES), lambda i: (i % n_tab, 0)),
            pl.BlockSpec((tm, LANES), lambda i: (i % n_tab, 0)),
        ],
        out_specs=pl.BlockSpec((tm, LANES), lambda i: (i, 0)),
        out_shape=jax.ShapeDtypeStruct((n, LANES), F32),
        compiler_params=_cparams("parallel"),
        name="small_proj_" + kind,
    )(x, g.reshape(1, d), w, b, cos, sin)


def _wo_kernel(a_ref, w_ref, r_ref, o_ref):
    o_ref[...] = r_ref[...] + jnp.dot(a_ref[...], w_ref[...], preferred_element_type=F32)


def _wo(a, w, res, tm_pref=512):
    n, d = res.shape
    tm = _tile(n, tm_pref)
    return pl.pallas_call(
        _wo_kernel,
        grid=(n // tm,),
        in_specs=[
            pl.BlockSpec((tm, d), lambda i: (i, 0)),
            pl.BlockSpec((d, d), lambda i: (0, 0)),
            pl.BlockSpec((tm, d), lambda i: (i, 0)),
        ],
        out_specs=pl.BlockSpec((tm, d), lambda i: (i, 0)),
        out_shape=jax.ShapeDtypeStruct((n, d), F32),
        compiler_params=_cparams("parallel"),
        name="wo",
    )(a, w, res)


def _ffn_kernel(nf, x_ref, g_ref, wu_ref, wd_ref, o_ref, xn_ref, acc_ref):
    f = pl.program_id(1)

    @pl.when(f == 0)
    def _():
        xn_ref[...] = _rms(x_ref[...], g_ref[...], NORM_EPS).astype(BF16)
        acc_ref[...] = jnp.zeros_like(acc_ref)

    u = jnp.dot(xn_ref[...], wu_ref[...], preferred_element_type=F32)
    u = jnp.maximum(u, 0.0)
    acc_ref[...] += jnp.dot((u * u).astype(BF16), wd_ref[...], preferred_element_type=F32)

    @pl.when(f == nf - 1)
    def _():
        o_ref[...] = x_ref[...] + acc_ref[...]


def _ffn(x, g, w_up, w_down, tm_pref=1024, tf=512):
    n, d = x.shape
    dff = w_up.shape[1]
    tm = _tile(n, tm_pref)
    nf = dff // tf
    return pl.pallas_call(
        functools.partial(_ffn_kernel, nf),
        grid=(n // tm, nf),
        in_specs=[
            pl.BlockSpec((tm, d), lambda i, f: (i, 0)),
            pl.BlockSpec((1, d), lambda i, f: (0, 0)),
            pl.BlockSpec((d, tf), lambda i, f: (0, f)),
            pl.BlockSpec((tf, d), lambda i, f: (f, 0)),
        ],
        out_specs=pl.BlockSpec((tm, d), lambda i, f: (i, 0)),
        out_shape=jax.ShapeDtypeStruct((n, d), F32),
        scratch_shapes=[pltpu.VMEM((tm, d), BF16), pltpu.VMEM((tm, d), F32)],
        compiler_params=_cparams("parallel", "arbitrary"),
        name="ffn",
    )(x, g.reshape(1, d), w_up, w_down)


def _final_norm_kernel(x_ref, g_ref, o_ref):
    o_ref[...] = _rms(x_ref[...], g_ref[...], NORM_EPS)


def _final_norm(x, g, tm_pref=512):
    n, d = x.shape
    tm = _tile(n, tm_pref)
    return pl.pallas_call(
        _final_norm_kernel,
        grid=(n // tm,),
        in_specs=[pl.BlockSpec((tm, d), lambda i: (i, 0)), pl.BlockSpec((1, d), lambda i: (0, 0))],
        out_specs=pl.BlockSpec((tm, d), lambda i: (i, 0)),
        out_shape=jax.ShapeDtypeStruct((n, d), F32),
        compiler_params=_cparams("parallel"),
        name="final_norm",
    )(x, g.reshape(1, d))


def _cumsum_kernel(x_ref, o_ref):
    x = x_ref[0]
    lane = lax.broadcasted_iota(I32, x.shape, 1)
    d = 1
    while d < x.shape[1]:
        x = x + jnp.where(lane >= d, pltpu.roll(x, d, 1), 0.0)
        d *= 2
    o_ref[0] = x


def _cumsum_time(x):
    b, h, t = x.shape
    return pl.pallas_call(
        _cumsum_kernel,
        grid=(b,),
        in_specs=[pl.BlockSpec((1, h, t), lambda i: (i, 0, 0))],
        out_specs=pl.BlockSpec((1, h, t), lambda i: (i, 0, 0)),
        out_shape=jax.ShapeDtypeStruct((b, h, t), F32),
        compiler_params=_cparams("parallel"),
        name="cumsum_time",
    )(x)


def _qk(q, k):
    return lax.dot_general(q, k, (((1,), (1,)), ((), ())), preferred_element_type=F32)


def _attn_scratch(n_heads, rows, width):
    return [
        pltpu.VMEM((n_heads, rows, LANES), BF16),
        pltpu.VMEM((n_heads, rows, LANES), F32),
        pltpu.VMEM((n_heads, rows, width), F32),
    ]


def _attn_start(q_ref, n_pairs, qm_ref, m_ref, acc_ref):
    rows = q_ref.shape[1]
    lane = lax.broadcasted_iota(I32, (rows, LANES), 1)
    for pp in range(n_pairs):
        q2 = q_ref[0, :, pp * LANES:(pp + 1) * LANES]
        qm_ref[2 * pp] = jnp.where(lane < HEAD_DIM, q2, jnp.zeros_like(q2))
        qm_ref[2 * pp + 1] = jnp.where(lane >= HEAD_DIM, q2, jnp.zeros_like(q2))
    m_ref[...] = jnp.full(m_ref.shape, NEG, F32)
    acc_ref[...] = jnp.zeros(acc_ref.shape, F32)


def _head_column(cq_blk, h):
    lane = lax.broadcasted_iota(I32, cq_blk.shape, 1)
    return jnp.sum(jnp.where(lane == h, cq_blk, 0.0), axis=1, keepdims=True)


def _block_update(mode, n_pairs, gp0, get_kv, keep, cq_blk, ck_ref, bias, qm_ref, m_ref, acc_ref, depth):
    kv = {}

    def pair(pp):
        if pp not in kv:
            kv[pp] = get_kv(pp)
        return kv[pp]

    def scores(hh):
        pp, j = divmod(hh, 2)
        k2, _ = pair(pp)
        s = _qk(qm_ref[hh], k2)
        if mode == "fox":
            h = 2 * (gp0 + pp) + j
            s = s + _head_column(cq_blk, h) - ck_ref[0, pl.ds(h, 1), :][:, :k2.shape[0]]
        if bias is not None:
            s = s + bias
        if keep is not None:
            s = jnp.where(keep, s, NEG)
        return s

    def finish(hh, s):
        pp, j = divmod(hh, 2)
        _, v2 = pair(pp)
        keys = v2.shape[0]
        ones = jnp.ones_like(v2)
        if mode == "diff":
            vj = jnp.concatenate([v2, ones], axis=1)
        else:
            lane_v = lax.broadcasted_iota(I32, v2.shape, 1)
            vj = jnp.where((lane_v >= HEAD_DIM) if j else (lane_v < HEAD_DIM), v2, ones)
        m_prev = m_ref[hh]
        m_new = jnp.maximum(m_prev, jnp.max(s, axis=1, keepdims=True))
        alpha = jnp.exp(m_prev - m_new)
        if keys % LANES == 0:
            p = jnp.concatenate([jnp.exp(s[:, c * LANES:(c + 1) * LANES] - m_new)
                                 for c in range(keys // LANES)], axis=1)
        else:
            p = jnp.exp(s - m_new[:, :1])
        pv = jnp.dot(p.astype(BF16), vj, preferred_element_type=F32)
        if mode == "diff":
            alpha = jnp.concatenate([alpha, alpha], axis=1)
        acc_ref[hh] = alpha * acc_ref[hh] + pv
        m_ref[hh] = m_new

    n_heads = 2 * n_pairs
    pending = []
    for hh in range(n_heads + depth):
        if hh < n_heads:
            pending.append((hh, scores(hh)))
        if hh >= depth:
            finish(*pending.pop(0))


def _attn_finalize(mode, lam_init, lam_ref, g_ref, o_ref, n_pairs, acc_ref):
    rows = acc_ref.shape[1]
    lane = lax.broadcasted_iota(I32, (rows, LANES), 1)
    for pp in range(n_pairs):
        a0, a1 = acc_ref[2 * pp], acc_ref[2 * pp + 1]
        if mode == "diff":
            lp = lam_ref[...]
            lam = (jnp.exp(jnp.sum(lp[0:1] * lp[1:2], axis=1, keepdims=True))
                   - jnp.exp(jnp.sum(lp[2:3] * lp[3:4], axis=1, keepdims=True)) + lam_init)
            o = a0[:, :LANES] / a0[:, LANES:] - lam * (a1[:, :LANES] / a1[:, LANES:])
            o = _rms(o, g_ref[...], SUBLN_EPS) * (1.0 - lam_init)
        else:
            o = jnp.where(lane < HEAD_DIM, a0 / pltpu.roll(a0, HEAD_DIM, 1), a1 / pltpu.roll(a1, HEAD_DIM, 1))
        o_ref[0, :, pp * LANES:(pp + 1) * LANES] = o.astype(o_ref.dtype)


def _attn_prompt_kernel(mode, tq, tk, nk, n_pairs, lam_init, *refs):
    q_ref, k_ref, v_ref = refs[:3]
    cq_ref = ck_ref = bias_ref = lam_ref = g_ref = None
    if mode == "fox":
        cq_ref, ck_ref = refs[3:5]
        rest = refs[5:]
    elif mode == "diff":
        lam_ref, g_ref = refs[3:5]
        rest = refs[5:]
    else:
        bias_ref = refs[3]
        rest = refs[4:]
    o_ref, qm_ref, m_ref, acc_ref = rest
    hp, qi, ki = pl.program_id(1), pl.program_id(2), pl.program_id(3)

    @pl.when(ki == 0)
    def _():
        _attn_start(q_ref, n_pairs, qm_ref, m_ref, acc_ref)

    q_lo, k_lo = qi * tq, ki * tk
    needed = k_lo <= q_lo + tq - 1
    straddle = k_lo + tk - 1 > q_lo

    def process(masked):
        keep = None
        if masked:
            rows = q_lo + lax.broadcasted_iota(I32, (tq, tk), 0)
            cols = k_lo + lax.broadcasted_iota(I32, (tq, tk), 1)
            keep = cols <= rows if mode == "fox" else (cols >> CHUNK_SHIFT) <= (rows >> CHUNK_SHIFT)
        bias = bias_ref[0, 0].astype(F32) if mode == "dsa" else None
        cq_blk = cq_ref[0] if mode == "fox" else None

        def get_kv(pp):
            sl = slice(pp * LANES, (pp + 1) * LANES)
            return k_ref[0, :, sl], v_ref[0, :, sl]

        _block_update(mode, n_pairs, hp * n_pairs, get_kv, keep, cq_blk, ck_ref, bias,
                      qm_ref, m_ref, acc_ref, PROMPT_DEPTH)

    if mode == "dsa":
        pl.when(needed)(lambda: process(False))
    else:
        pl.when(needed & straddle)(lambda: process(True))
        pl.when(needed & jnp.logical_not(straddle))(lambda: process(False))

    @pl.when(ki == nk - 1)
    def _():
        _attn_finalize(mode, lam_init, lam_ref, g_ref, o_ref, n_pairs, acc_ref)


def _attn_prompt(mode, q, k, v, extras, lam_init=0.0, tq_pref=512, tk_pref=512, n_pairs=4):
    b, t, d = q.shape
    tq, tk = _tile(t, tq_pref), _tile(t, tk_pref)
    nq, nk = t // tq, t // tk
    wl = LANES * n_pairs
    width = 2 * LANES if mode == "diff" else LANES

    def kmax(qi):
        return (qi * tq + tq - 1) // tk

    in_specs = [
        pl.BlockSpec((1, tq, wl), lambda bi, hp, qi, ki: (bi, qi, hp)),
        pl.BlockSpec((1, tk, wl), lambda bi, hp, qi, ki: (bi, jnp.minimum(ki, kmax(qi)), hp)),
        pl.BlockSpec((1, tk, wl), lambda bi, hp, qi, ki: (bi, jnp.minimum(ki, kmax(qi)), hp)),
    ]
    if mode == "fox":
        nh = extras[0].shape[2]
        in_specs += [
            pl.BlockSpec((1, tq, nh), lambda bi, hp, qi, ki: (bi, qi, 0)),
            pl.BlockSpec((1, nh, tk), lambda bi, hp, qi, ki: (bi, 0, jnp.minimum(ki, kmax(qi)))),
        ]
    elif mode == "diff":
        in_specs += [
            pl.BlockSpec((4, HEAD_DIM), lambda bi, hp, qi, ki: (0, 0)),
            pl.BlockSpec((1, LANES), lambda bi, hp, qi, ki: (0, 0)),
        ]
    else:
        in_specs += [
            pl.BlockSpec((1, 1, tq, tk), lambda bi, hp, qi, ki: (bi, jnp.minimum(ki, kmax(qi)), qi, 0)),
        ]
    return pl.pallas_call(
        functools.partial(_attn_prompt_kernel, mode, tq, tk, nk, n_pairs, lam_init),
        grid=(b, d // wl, nq, nk),
        in_specs=in_specs,
        out_specs=pl.BlockSpec((1, tq, wl), lambda bi, hp, qi, ki: (bi, qi, hp)),
        out_shape=jax.ShapeDtypeStruct((b, t, d), BF16),
        scratch_shapes=_attn_scratch(2 * n_pairs, tq, width),
        compiler_params=_cparams("parallel", "parallel", "parallel", "arbitrary"),
        name="attn_prompt_" + mode,
    )(q, k, v, *extras)


def _attn_sample_kernel(mode, ts, tk, nkp, n_pairs, lam_init, *refs):
    q_ref, pk_ref, pv_ref, nk_ref, nv_ref = refs[:5]
    cq_ref = ckp_ref = ckn_ref = bp_ref = bn_ref = lam_ref = g_ref = None
    if mode == "fox":
        cq_ref, ckp_ref, ckn_ref = refs[5:8]
        rest = refs[8:]
    elif mode == "diff":
        lam_ref, g_ref = refs[5:7]
        rest = refs[7:]
    else:
        bp_ref, bn_ref = refs[5:7]
        rest = refs[7:]
    o_ref, qm_ref, m_ref, acc_ref = rest
    ki = pl.program_id(1)

    @pl.when(ki == 0)
    def _():
        _attn_start(q_ref, n_pairs, qm_ref, m_ref, acc_ref)

    cq_blk = cq_ref[0] if mode == "fox" else None
    bias = bp_ref[0, 0].astype(F32) if mode == "dsa" else None

    def past_kv(pp):
        sl = slice(pp * LANES, (pp + 1) * LANES)
        return pk_ref[0, 0, :, sl], pv_ref[0, 0, :, sl]

    _block_update(mode, n_pairs, 0, past_kv, None, cq_blk, ckp_ref, bias, qm_ref, m_ref, acc_ref, SAMPLE_DEPTH)

    @pl.when(ki == nkp - 1)
    def _():
        keep = None
        if mode == "fox":
            keep = lax.broadcasted_iota(I32, (ts, ts), 1) <= lax.broadcasted_iota(I32, (ts, ts), 0)
        bias_n = bn_ref[0, 0][:, :ts].astype(F32) if mode == "dsa" else None

        def new_kv(pp):
            sl = slice(pp * LANES, (pp + 1) * LANES)
            return nk_ref[0, :, sl], nv_ref[0, :, sl]

        _block_update(mode, n_pairs, 0, new_kv, keep, cq_blk, ckn_ref, bias_n, qm_ref, m_ref, acc_ref, SAMPLE_DEPTH)
        _attn_finalize(mode, lam_init, lam_ref, g_ref, o_ref, n_pairs, acc_ref)


def _attn_sample(mode, layer, q, past_k, past_v, new_k, new_v, extras, lam_init=0.0, tk_pref=512):
    b, ts, d = q.shape
    p = past_k.shape[2]
    tk = _tile(p, tk_pref)
    nkp = p // tk
    n_pairs = d // LANES
    width = 2 * LANES if mode == "diff" else LANES
    in_specs = [
        pl.BlockSpec((1, ts, d), lambda bi, ki: (bi, 0, 0)),
        pl.BlockSpec((1, 1, tk, d), lambda bi, ki: (layer, bi, ki, 0)),
        pl.BlockSpec((1, 1, tk, d), lambda bi, ki: (layer, bi, ki, 0)),
        pl.BlockSpec((1, ts, d), lambda bi, ki: (bi, 0, 0)),
        pl.BlockSpec((1, ts, d), lambda bi, ki: (bi, 0, 0)),
    ]
    if mode == "fox":
        nh = extras[0].shape[2]
        in_specs += [
            pl.BlockSpec((1, ts, nh), lambda bi, ki: (bi, 0, 0)),
            pl.BlockSpec((1, nh, tk), lambda bi, ki: (bi, 0, ki)),
            pl.BlockSpec((1, nh, LANES), lambda bi, ki: (bi, 0, p // LANES)),
        ]
    elif mode == "diff":
        in_specs += [
            pl.BlockSpec((4, HEAD_DIM), lambda bi, ki: (0, 0)),
            pl.BlockSpec((1, LANES), lambda bi, ki: (0, 0)),
        ]
    else:
        in_specs += [
            pl.BlockSpec((1, 1, ts, tk), lambda bi, ki: (bi, ki, 0, 0)),
            pl.BlockSpec((1, 1, ts, tk), lambda bi, ki: (bi, nkp, 0, 0)),
        ]
    return pl.pallas_call(
        functools.partial(_attn_sample_kernel, mode, ts, tk, nkp, n_pairs, lam_init),
        grid=(b, nkp),
        in_specs=in_specs,
        out_specs=pl.BlockSpec((1, ts, d), lambda bi, ki: (bi, 0, 0)),
        out_shape=jax.ShapeDtypeStruct((b, ts, d), BF16),
        scratch_shapes=_attn_scratch(2 * n_pairs, ts, width),
        compiler_params=_cparams("parallel", "arbitrary"),
        name="attn_sample_" + mode,
    )(q, past_k, past_v, new_k, new_v, *extras)


def _dsa_select_kernel(tq, tk, nkb, top_k, q_pos0, n_valid, causal, qi_ref, w_ref, kk_ref, tri_ref, bias_ref, key_ref):
    i = pl.program_id(1)
    nb = (i * tq + tq - 1) // tk + 1 if causal else nkb
    q_chunk = (q_pos0 + i * tq + lax.broadcasted_iota(I32, (tq, tk), 0)) >> CHUNK_SHIFT
    col = lax.broadcasted_iota(I32, (tq, tk), 1)

    def score_block(kb, carry):
        kblk = kk_ref[0, pl.ds(pl.multiple_of(kb * tk, tk), tk), :][:, :IDX_DIM].astype(BF16)
        sc = jnp.zeros((tq, tk), F32)
        for h in range(N_IDX_HEADS):
            s = _qk(qi_ref[0, :, h * IDX_DIM:(h + 1) * IDX_DIM], kblk)
            sc = sc + jnp.maximum(s, 0.0) * w_ref[0, :, IDX_DIM + h:IDX_DIM + h + 1]
        sc = jnp.where(sc == 0.0, 0.0, sc)
        kidx = kb * tk + col
        adm = ((kidx >> CHUNK_SHIFT) <= q_chunk) & (kidx < n_valid)
        bits = pltpu.bitcast(sc, I32)
        key = jnp.where(bits >= 0, bits, bits ^ jnp.int32(0x7FFFFFFF))
        key_ref[kb] = jnp.where(adm, key, jnp.int32(INT_MIN))
        return carry

    lax.fori_loop(0, nb, score_block, 0)

    def count(pred):
        def body(kb, acc):
            key = key_ref[kb]
            for c in range(tk // LANES):
                acc = acc + jnp.where(pred(key[:, c * LANES:(c + 1) * LANES]), 1.0, 0.0)
            return acc
        acc = lax.fori_loop(0, nb, body, jnp.zeros((tq, LANES), F32))
        return jnp.sum(acc, axis=1, keepdims=True)

    def bit_body(it, u):
        cand = u | lax.shift_left(jnp.int32(1), 31 - it)
        cand_b = jnp.broadcast_to(cand ^ jnp.int32(INT_MIN), (tq, LANES))
        cnt = count(lambda key: key >= cand_b)
        return jnp.where(cnt >= top_k, cand, u)

    u = lax.fori_loop(0, 32, bit_body, jnp.zeros((tq, 1), I32))
    thr = u ^ jnp.int32(INT_MIN)
    thr_b = jnp.broadcast_to(thr, (tq, LANES))
    need = top_k - count(lambda key: key > thr_b)

    def out_body(kb, seen):
        key = key_ref[kb]
        eq = (key == thr) & (key > jnp.int32(INT_MIN))
        eqf = jnp.where(eq, 1.0, 0.0)
        rank = seen + jnp.dot(eqf.astype(BF16), tri_ref[...], preferred_element_type=F32)
        sel = (key > thr) | (eq & (rank < need))
        bias_ref[0, kb] = jnp.where(sel, 0.0, NEG).astype(bias_ref.dtype)
        return seen + jnp.sum(eqf, axis=1, keepdims=True)

    lax.fori_loop(0, nb, out_body, jnp.zeros((tq, 1), F32))

    def fill(kb, carry):
        bias_ref[0, kb] = jnp.full((tq, tk), NEG, bias_ref.dtype)
        return carry

    lax.fori_loop(nb, nkb, fill, 0)


def _dsa_select(qidx, wsrc, keys, q_pos0, n_valid, causal, tq, tk):
    b, t, _ = qidx.shape
    s, kd = keys.shape[1], keys.shape[2]
    nkb = s // tk
    top_k = min(TOPK_MAX, n_valid // 4)
    tri = (jnp.arange(tk)[:, None] < jnp.arange(tk)[None, :]).astype(BF16)
    return pl.pallas_call(
        functools.partial(_dsa_select_kernel, tq, tk, nkb, top_k, q_pos0, n_valid, causal),
        grid=(b, t // tq),
        in_specs=[
            pl.BlockSpec((1, tq, N_IDX_HEADS * IDX_DIM), lambda bi, i: (bi, i, 0)),
            pl.BlockSpec((1, tq, LANES), lambda bi, i: (bi, i, 0)),
            pl.BlockSpec((1, s, kd), lambda bi, i: (bi, 0, 0)),
            pl.BlockSpec((tk, tk), lambda bi, i: (0, 0)),
        ],
        out_specs=pl.BlockSpec((1, nkb, tq, tk), lambda bi, i: (bi, 0, i, 0)),
        out_shape=jax.ShapeDtypeStruct((b, nkb, t, tk), BF16),
        scratch_shapes=[pltpu.VMEM((nkb, tq, tk), I32)],
        compiler_params=_cparams("parallel", "arbitrary"),
        name="dsa_select",
    )(qidx, wsrc, keys, tri)


def _pad_cols(w, n):
    return jnp.pad(w, ((0, 0), (0, n - w.shape[1])))


def _diff_lambda_init(layer):
    return 0.8 - 0.6 * math.exp(-0.3 * layer)


def _fox_layer(j, g, hp, hs, bp, bs, cache_k, cache_v, cache_logf, w_qkv, w_f, b_f, w_o, tabs_p, tabs_s):
    d = D_MODEL
    nh = w_f.shape[1]
    w = w_qkv.astype(BF16)
    outs = [(0, d, False, ATTN_SCALE, BF16), (d, d, False, 1.0, F32), (2 * d, d, False, 1.0, F32),
            (d, d, False, 1.0, BF16), (2 * d, d, False, 1.0, BF16)]
    wf = _pad_cols(w_f, LANES)
    bf = _pad_cols(b_f.reshape(1, nh), LANES)
    res = []
    for h, nb, tabs in ((hp, bp, tabs_p), (hs, bs, tabs_s)):
        q, k, v, kb, vb = (a.reshape(nb, -1, d) for a in _proj(h, g, w, outs, *tabs))
        logf = _small_proj("fgate", h, g, wf, bf, *tabs)[:, :nh]
        res.append((q, k, v, kb, vb, logf.reshape(nb, -1, nh)))
    (qp, kp, vp, kbp, vbp, lfp), (qs, ks, vs, kbs, vbs, lfs) = res

    cum_row = _cumsum_time(jnp.swapaxes(lfp, 1, 2))
    op = _attn_prompt("fox", qp, kbp, vbp, (jnp.swapaxes(cum_row, 1, 2), cum_row))

    past = cache_logf.shape[2]
    ts = lfs.shape[1]
    lf_all = jnp.swapaxes(jnp.concatenate([cache_logf[j], lfs], axis=1), 1, 2)
    padded = -(-(past + ts) // LANES) * LANES
    cum_s = _cumsum_time(jnp.pad(lf_all, ((0, 0), (0, 0), (0, padded - past - ts))))
    cq_s = jnp.swapaxes(cum_s[:, :, past:past + ts], 1, 2)
    os_ = _attn_sample("fox", j, qs, cache_k, cache_v, kbs, vbs, (cq_s, cum_s, cum_s))

    wo = w_o.astype(BF16)
    hp = _wo(op.reshape(-1, d), wo, hp)
    hs = _wo(os_.reshape(-1, d), wo, hs)
    return hp, hs, (kp, vp, lfp), (ks, vs, lfs)


def _diff_layer(layer, j, g, hp, hs, bp, bs, cache_k, cache_v, w_qkv, lams, subln_g, w_o, tabs_p, tabs_s):
    d = D_MODEL
    w = w_qkv.astype(BF16)
    outs = [(0, d, True, ATTN_SCALE, BF16), (d, d, True, 1.0, F32), (2 * d, d, False, 1.0, F32),
            (d, d, True, 1.0, BF16), (2 * d, d, False, 1.0, BF16)]
    lam_init = _diff_lambda_init(layer)
    extras = (lams, subln_g.reshape(1, LANES))
    qp, kp, vp, kbp, vbp = (a.reshape(bp, -1, d) for a in _proj(hp, g, w, outs, *tabs_p))
    qs, ks, vs, kbs, vbs = (a.reshape(bs, -1, d) for a in _proj(hs, g, w, outs, *tabs_s))
    op = _attn_prompt("diff", qp, kbp, vbp, extras, lam_init)
    os_ = _attn_sample("diff", j, qs, cache_k, cache_v, kbs, vbs, extras, lam_init)
    wo = w_o.astype(BF16)
    hp = _wo(op.reshape(-1, d), wo, hp)
    hs = _wo(os_.reshape(-1, d), wo, hs)
    return hp, hs, (kp, vp), (ks, vs)


def _dsa_layer(j, g, hp, hs, bp, bs, cache_k, cache_v, cache_kidx, w_qkv, w_qidx, w_kidx, w_widx, w_o, tabs_p, tabs_s):
    d = D_MODEL
    ni = N_IDX_HEADS * IDX_DIM
    w = jnp.concatenate([w_qkv, w_qidx], axis=1).astype(BF16)
    outs = [(0, d, True, ATTN_SCALE, BF16), (d, d, True, 1.0, F32), (2 * d, d, False, 1.0, F32),
            (3 * d, ni, True, IDX_SCALE, BF16), (d, d, True, 1.0, BF16), (2 * d, d, False, 1.0, BF16)]
    w_small = _pad_cols(jnp.concatenate([w_kidx, w_widx], axis=1), LANES)
    zero_b = jnp.zeros((1, LANES), F32)

    qp, kp, vp, qip, kbp, vbp = (a.reshape(bp, -1, a.shape[-1]) for a in _proj(hp, g, w, outs, *tabs_p))
    spp = _small_proj("dsa", hp, g, w_small, zero_b, *tabs_p).reshape(bp, -1, LANES)
    t = qp.shape[1]
    sel_tk = _tile(t, 512)
    bias_p = _dsa_select(qip, spp, spp, 0, t, True, _tile(t, 128), sel_tk)
    op = _attn_prompt("dsa", qp, kbp, vbp, (bias_p,), tq_pref=sel_tk, tk_pref=sel_tk)

    qs, ks, vs, qis, kbs, vbs = (a.reshape(bs, -1, a.shape[-1]) for a in _proj(hs, g, w, outs, *tabs_s))
    sps = _small_proj("dsa", hs, g, w_small, zero_b, *tabs_s).reshape(bs, -1, LANES)
    ts = qs.shape[1]
    past = cache_kidx.shape[2]
    tk_s = _tile(past, 512)
    keys_s = jnp.concatenate([cache_kidx[j], sps[:, :, :IDX_DIM]], axis=1)
    keys_s = jnp.pad(keys_s, ((0, 0), (0, tk_s - ts), (0, 0)))
    bias_s = _dsa_select(qis, sps, keys_s, past, past + ts, False, ts, tk_s)
    os_ = _attn_sample("dsa", j, qs, cache_k, cache_v, kbs, vbs, (bias_s, bias_s), tk_pref=tk_s)

    wo = w_o.astype(BF16)
    hp = _wo(op.reshape(-1, d), wo, hp)
    hs = _wo(os_.reshape(-1, d), wo, hs)
    return hp, hs, (kp, vp, spp[:, :, :IDX_DIM]), (ks, vs, sps[:, :, :IDX_DIM])


def kernel(x_prompt, x_sample, cache_a_k, cache_a_v, cache_a_logf, cache_b_k, cache_b_v, cache_c_k, cache_c_v, cache_c_kidx, norm_mix_g, norm_ffn_g, norm_final_g, a_w_qkv, a_w_f, a_b_f, a_w_o, b_w_qkv, b_lam_q1, b_lam_k1, b_lam_q2, b_lam_k2, b_subln_g, b_w_o, c_w_qkv, c_w_qidx, c_w_kidx, c_w_widx, c_w_o, ffn_w_up, ffn_w_down):
    bp, tp, d = x_prompt.shape
    bs, ts, _ = x_sample.shape
    past = cache_a_k.shape[2]
    depth = norm_mix_g.shape[0]
    tabs_p = _rope_tables(jnp.arange(tp, dtype=I32))
    cos_s, sin_s = _rope_tables(past + jnp.arange(ts, dtype=I32))
    tabs_s = (jnp.tile(cos_s, (bs, 1)), jnp.tile(sin_s, (bs, 1)))

    def flat_bf16(cache):
        return cache.reshape(*cache.shape[:3], d).astype(BF16)

    cache_a_k, cache_a_v, cache_b_k, cache_b_v, cache_c_k, cache_c_v = (
        flat_bf16(c) for c in (cache_a_k, cache_a_v, cache_b_k, cache_b_v, cache_c_k, cache_c_v))

    hp = x_prompt.reshape(bp * tp, d)
    hs = x_sample.reshape(bs * ts, d)
    new = {"a": ([], []), "b": ([], []), "c": ([], [])}
    for i in range(depth):
        kind, j = i % N_MIXERS, i // N_MIXERS
        g = norm_mix_g[i]
        if kind == 0:
            hp, hs, sp, ss = _fox_layer(j, g, hp, hs, bp, bs, cache_a_k, cache_a_v, cache_a_logf,
                                        a_w_qkv[j], a_w_f[j], a_b_f[j], a_w_o[j], tabs_p, tabs_s)
            key = "a"
        elif kind == 1:
            lams = jnp.stack([b_lam_q1[j], b_lam_k1[j], b_lam_q2[j], b_lam_k2[j]])
            hp, hs, sp, ss = _diff_layer(i, j, g, hp, hs, bp, bs, cache_b_k, cache_b_v,
                                         b_w_qkv[j], lams, b_subln_g[j], b_w_o[j], tabs_p, tabs_s)
            key = "b"
        else:
            hp, hs, sp, ss = _dsa_layer(j, g, hp, hs, bp, bs, cache_c_k, cache_c_v, cache_c_kidx,
                                        c_w_qkv[j], c_w_qidx[j], c_w_kidx[j], c_w_widx[j], c_w_o[j], tabs_p, tabs_s)
            key = "c"
        new[key][0].append(sp)
        new[key][1].append(ss)
        wu, wd = ffn_w_up[i].astype(BF16), ffn_w_down[i].astype(BF16)
        hp = _ffn(hp, norm_ffn_g[i], wu, wd)
        hs = _ffn(hs, norm_ffn_g[i], wu, wd)

    y_prompt = _final_norm(hp, norm_final_g).reshape(bp, tp, d)
    y_sample = _final_norm(hs, norm_final_g).reshape(bs, ts, d)

    def stack(items, idx, shape_tail):
        return jnp.stack([s[idx].reshape(*s[idx].shape[:2], *shape_tail) for s in items])

    nha = d // HEAD_DIM
    outs = [y_prompt, y_sample]
    for side in (0, 1):
        a, b, c = new["a"][side], new["b"][side], new["c"][side]
        outs += [
            stack(a, 0, (nha, HEAD_DIM)), stack(a, 1, (nha, HEAD_DIM)), stack(a, 2, (nha,)),
            stack(b, 0, (nha, HEAD_DIM)), stack(b, 1, (nha // 2, 2 * HEAD_DIM)),
            stack(c, 0, (nha, HEAD_DIM)), stack(c, 1, (nha, HEAD_DIM)), stack(c, 2, (IDX_DIM,)),
        ]
    return tuple(outs)
```

```python
import functools
import math

import jax
import jax.numpy as jnp
from jax import lax
from jax.experimental import pallas as pl
from jax.experimental.pallas import tpu as pltpu

F32 = jnp.float32
BF16 = jnp.bfloat16
I32 = jnp.int32

D_MODEL = 1024
HEAD_DIM = 64
CHUNK_SHIFT = 6
N_IDX_HEADS = 8
IDX_DIM = 64
TOPK_MAX = 256
N_MIXERS = 3
ROPE_THETA = 10000.0
NORM_EPS = 1e-6
SUBLN_EPS = 1e-5
ATTN_SCALE = HEAD_DIM ** -0.5
IDX_SCALE = IDX_DIM ** -0.5
IDX_HEAD_SCALE = N_IDX_HEADS ** -0.5
LOG2E = math.log2(math.e)
Q_SCALE = ATTN_SCALE * LOG2E

LANES = 128
NEG = -1e30
INT_MIN = -(2 ** 31)
VMEM_LIMIT_BYTES = 48 * 1024 * 1024
PROMPT_DEPTH = 1
SAMPLE_DEPTH = 16


def _cparams(*sem):
    return pltpu.CompilerParams(dimension_semantics=sem, vmem_limit_bytes=VMEM_LIMIT_BYTES)


def _tile(n, pref):
    return pref if n % pref == 0 else n


def _rms(x, g, eps):
    ms = jnp.mean(x * x, axis=-1, keepdims=True)
    return x * lax.rsqrt(ms + eps) * g


def _rope_lanes(y, cos, sin_signed, first_half):
    partner = jnp.where(first_half, pltpu.roll(y, 96, 1), pltpu.roll(y, 32, 1))
    return y * cos + partner * sin_signed


def _rope_tables(pos):
    half = HEAD_DIM // 2
    inv = ROPE_THETA ** (-jnp.arange(half, dtype=F32) / half)
    ang = pos.astype(F32)[:, None] * inv[None, :]
    cos, sin = jnp.cos(ang), jnp.sin(ang)
    cos_h = jnp.concatenate([cos, cos], axis=-1)
    sin_h = jnp.concatenate([-sin, sin], axis=-1)
    return jnp.tile(cos_h, (1, 2)), jnp.tile(sin_h, (1, 2))


def _proj_kernel(groups, n_out, tn, x_ref, g_ref, w_ref, cos_ref, sin_ref, *rest):
    out_refs, xn_ref = rest[:n_out], rest[n_out]
    j = pl.program_id(1)

    @pl.when(j == 0)
    def _():
        xn_ref[...] = _rms(x_ref[...], g_ref[...], NORM_EPS).astype(BF16)

    for (lo, hi), members in groups:

        @pl.when((j >= lo) & (j < hi))
        def _(members=members):
            acc = jnp.dot(xn_ref[...], w_ref[...], preferred_element_type=F32)
            roped = None
            for idx, rope, scale in members:
                o_ref = out_refs[idx]
                if rope and roped is None:
                    cos, sin = cos_ref[...], sin_ref[...]
                    first = (lax.broadcasted_iota(I32, cos.shape, 1) % HEAD_DIM) < HEAD_DIM // 2
                    roped = jnp.concatenate(
                        [_rope_lanes(acc[:, c * LANES:(c + 1) * LANES], cos, sin, first) for c in range(tn // LANES)], axis=1)
                y = roped if rope else acc
                o_ref[...] = (y * scale).astype(o_ref.dtype)


def _proj(x, g, w, outs, cos, sin, tm_pref=1024, tn=512):
    n, d = x.shape
    tm = _tile(n, tm_pref)
    t_tab = cos.shape[0]
    n_tab = t_tab // tm
    groups, out_shapes, out_specs = {}, [], []
    for idx, (start, cols, rope, scale, dtype) in enumerate(outs):
        lo, nt = start // tn, cols // tn
        groups.setdefault((lo, lo + nt), []).append((idx, rope, scale))
        out_shapes.append(jax.ShapeDtypeStruct((n, cols), dtype))
        out_specs.append(pl.BlockSpec((tm, tn), functools.partial(
            lambda i, j, lo, nt: (i, jnp.clip(j - lo, 0, nt - 1)), lo=lo, nt=nt)))
    return pl.pallas_call(
        functools.partial(_proj_kernel, tuple((k, tuple(v)) for k, v in groups.items()), len(outs), tn),
        grid=(n // tm, w.shape[1] // tn),
        in_specs=[
            pl.BlockSpec((tm, d), lambda i, j: (i, 0)),
            pl.BlockSpec((1, d), lambda i, j: (0, 0)),
            pl.BlockSpec((d, tn), lambda i, j: (0, j)),
            pl.BlockSpec((tm, LANES), lambda i, j: (i % n_tab, 0)),
            pl.BlockSpec((tm, LANES), lambda i, j: (i % n_tab, 0)),
        ],
        out_specs=out_specs,
        out_shape=out_shapes,
        scratch_shapes=[pltpu.VMEM((tm, d), BF16)],
        compiler_params=_cparams("parallel", "arbitrary"),
        name="proj",
    )(x, g.reshape(1, d), w, cos, sin)


def _small_proj_kernel(kind, x_ref, g_ref, w_ref, b_ref, cos_ref, sin_ref, o_ref):
    xn = _rms(x_ref[...], g_ref[...], NORM_EPS)
    z = jnp.dot(xn, w_ref[...], preferred_element_type=F32, precision=lax.Precision.HIGHEST)
    if kind == "fgate":
        z = z + b_ref[...]
        o_ref[...] = jnp.minimum(z, 0.0) - jnp.log1p(jnp.exp(-jnp.abs(z)))
    else:
        lane = lax.broadcasted_iota(I32, z.shape, 1)
        first = (lane % HEAD_DIM) < HEAD_DIM // 2
        roped = _rope_lanes(z, cos_ref[...], sin_ref[...], first)
        o_ref[...] = jnp.where(lane < IDX_DIM, roped, z * IDX_HEAD_SCALE)


def _small_proj(kind, x, g, w, b, cos, sin, tm_pref=512):
    n, d = x.shape
    tm = _tile(n, tm_pref)
    n_tab = cos.shape[0] // tm
    return pl.pallas_call(
        functools.partial(_small_proj_kernel, kind),
        grid=(n // tm,),
        in_specs=[
            pl.BlockSpec((tm, d), lambda i: (i, 0)),
            pl.BlockSpec((1, d), lambda i: (0, 0)),
            pl.BlockSpec((d, LANES), lambda i: (0, 0)),
            pl.BlockSpec((1, LANES), lambda i: (0, 0)),
            pl.BlockSpec((tm, LANES), lambda i: (i % n_tab, 0)),
            pl.BlockSpec((tm, LANES), lambda i: (i % n_tab, 0)),
        ],
        out_specs=pl.BlockSpec((tm, LANES), lambda i: (i, 0)),
        out_shape=jax.ShapeDtypeStruct((n, LANES), F32),
        compiler_params=_cparams("parallel"),
        name="small_proj_" + kind,
    )(x, g.reshape(1, d), w, b, cos, sin)


def _wo_kernel(a_ref, w_ref, r_ref, o_ref):
    o_ref[...] = r_ref[...] + jnp.dot(a_ref[...], w_ref[...], preferred_element_type=F32)


def _wo(a, w, res, tm_pref=512):
    n, d = res.shape
    tm = _tile(n, tm_pref)
    return pl.pallas_call(
        _wo_kernel,
        grid=(n // tm,),
        in_specs=[
            pl.BlockSpec((tm, d), lambda i: (i, 0)),
            pl.BlockSpec((d, d), lambda i: (0, 0)),
            pl.BlockSpec((tm, d), lambda i: (i, 0)),
        ],
        out_specs=pl.BlockSpec((tm, d), lambda i: (i, 0)),
        out_shape=jax.ShapeDtypeStruct((n, d), F32),
        compiler_params=_cparams("parallel"),
        name="wo",
    )(a, w, res)


def _ffn_kernel(nf, x_ref, g_ref, wu_ref, wd_ref, o_ref, xn_ref, acc_ref):
    f = pl.program_id(1)

    @pl.when(f == 0)
    def _():
        xn_ref[...] = _rms(x_ref[...], g_ref[...], NORM_EPS).astype(BF16)
        acc_ref[...] = jnp.zeros_like(acc_ref)

    u = jnp.dot(xn_ref[...], wu_ref[...], preferred_element_type=F32)
    u = jnp.maximum(u, 0.0)
    acc_ref[...] += jnp.dot((u * u).astype(BF16), wd_ref[...], preferred_element_type=F32)

    @pl.when(f == nf - 1)
    def _():
        o_ref[...] = x_ref[...] + acc_ref[...]


def _ffn(x, g, w_up, w_down, tm_pref=1024, tf=512):
    n, d = x.shape
    dff = w_up.shape[1]
    tm = _tile(n, tm_pref)
    nf = dff // tf
    return pl.pallas_call(
        functools.partial(_ffn_kernel, nf),
        grid=(n // tm, nf),
        in_specs=[
            pl.BlockSpec((tm, d), lambda i, f: (i, 0)),
            pl.BlockSpec((1, d), lambda i, f: (0, 0)),
            pl.BlockSpec((d, tf), lambda i, f: (0, f)),
            pl.BlockSpec((tf, d), lambda i, f: (f, 0)),
        ],
        out_specs=pl.BlockSpec((tm, d), lambda i, f: (i, 0)),
        out_shape=jax.ShapeDtypeStruct((n, d), F32),
        scratch_shapes=[pltpu.VMEM((tm, d), BF16), pltpu.VMEM((tm, d), F32)],
        compiler_params=_cparams("parallel", "arbitrary"),
        name="ffn",
    )(x, g.reshape(1, d), w_up, w_down)


def _final_norm_kernel(x_ref, g_ref, o_ref):
    o_ref[...] = _rms(x_ref[...], g_ref[...], NORM_EPS)


def _final_norm(x, g, tm_pref=512):
    n, d = x.shape
    tm = _tile(n, tm_pref)
    return pl.pallas_call(
        _final_norm_kernel,
        grid=(n // tm,),
        in_specs=[pl.BlockSpec((tm, d), lambda i: (i, 0)), pl.BlockSpec((1, d), lambda i: (0, 0))],
        out_specs=pl.BlockSpec((tm, d), lambda i: (i, 0)),
        out_shape=jax.ShapeDtypeStruct((n, d), F32),
        compiler_params=_cparams("parallel"),
        name="final_norm",
    )(x, g.reshape(1, d))


def _cumsum_kernel(x_ref, o_ref):
    x = x_ref[0]
    lane = lax.broadcasted_iota(I32, x.shape, 1)
    d = 1
    while d < x.shape[1]:
        x = x + jnp.where(lane >= d, pltpu.roll(x, d, 1), 0.0)
        d *= 2
    o_ref[0] = x * LOG2E


def _cumsum_time(x):
    b, h, t = x.shape
    return pl.pallas_call(
        _cumsum_kernel,
        grid=(b,),
        in_specs=[pl.BlockSpec((1, h, t), lambda i: (i, 0, 0))],
        out_specs=pl.BlockSpec((1, h, t), lambda i: (i, 0, 0)),
        out_shape=jax.ShapeDtypeStruct((b, h, t), F32),
        compiler_params=_cparams("parallel"),
        name="cumsum_time",
    )(x)


def _qk(q, k):
    return lax.dot_general(q, k, (((1,), (1,)), ((), ())), preferred_element_type=F32)


def _attn_scratch(n_heads, rows, width):
    return [
        pltpu.VMEM((n_heads, rows, LANES), BF16),
        pltpu.VMEM((n_heads, rows, LANES), F32),
        pltpu.VMEM((n_heads, rows, width), F32),
    ]


def _attn_start(q_ref, n_pairs, qm_ref, m_ref, acc_ref):
    rows = q_ref.shape[1]
    lane = lax.broadcasted_iota(I32, (rows, LANES), 1)
    for pp in range(n_pairs):
        q2 = q_ref[0, :, pp * LANES:(pp + 1) * LANES]
        qm_ref[2 * pp] = jnp.where(lane < HEAD_DIM, q2, jnp.zeros_like(q2))
        qm_ref[2 * pp + 1] = jnp.where(lane >= HEAD_DIM, q2, jnp.zeros_like(q2))
    m_ref[...] = jnp.full(m_ref.shape, NEG, F32)
    acc_ref[...] = jnp.zeros(acc_ref.shape, F32)


def _head_column(cq_blk, h):
    lane = lax.broadcasted_iota(I32, cq_blk.shape, 1)
    return jnp.sum(jnp.where(lane == h, cq_blk, 0.0), axis=1, keepdims=True)


def _block_update(mode, n_pairs, gp0, get_kv, keep, cq_blk, ck_ref, bias, qm_ref, m_ref, acc_ref, depth):
    kv = {}

    def pair(pp):
        if pp not in kv:
            kv[pp] = get_kv(pp)
        return kv[pp]

    def scores(hh):
        pp, j = divmod(hh, 2)
        k2, _ = pair(pp)
        s = _qk(qm_ref[hh], k2)
        if mode == "fox":
            h = 2 * (gp0 + pp) + j
            s = s + _head_column(cq_blk, h) - ck_ref[0, pl.ds(h, 1), :][:, :k2.shape[0]]
        if bias is not None:
            s = s + bias
        if keep is not None:
            s = jnp.where(keep, s, NEG)
        return s

    def finish(hh, s):
        pp, j = divmod(hh, 2)
        _, v2 = pair(pp)
        keys = v2.shape[0]
        ones = jnp.ones_like(v2)
        if mode == "diff":
            vj = jnp.concatenate([v2, ones], axis=1)
        else:
            lane_v = lax.broadcasted_iota(I32, v2.shape, 1)
            vj = jnp.where((lane_v >= HEAD_DIM) if j else (lane_v < HEAD_DIM), v2, ones)
        m_prev = m_ref[hh]
        m_new = jnp.maximum(m_prev, jnp.max(s, axis=1, keepdims=True))
        alpha = jnp.exp2(m_prev - m_new)
        if keys % LANES == 0:
            p = jnp.concatenate([jnp.exp2(s[:, c * LANES:(c + 1) * LANES] - m_new)
                                 for c in range(keys // LANES)], axis=1)
        else:
            p = jnp.exp2(s - m_new[:, :1])
        pv = jnp.dot(p.astype(BF16), vj, preferred_element_type=F32)
        if mode == "diff":
            alpha = jnp.concatenate([alpha, alpha], axis=1)
        acc_ref[hh] = alpha * acc_ref[hh] + pv
        m_ref[hh] = m_new

    n_heads = 2 * n_pairs
    pending = []
    for hh in range(n_heads + depth):
        if hh < n_heads:
            pending.append((hh, scores(hh)))
        if hh >= depth:
            finish(*pending.pop(0))


def _attn_finalize(mode, lam_init, lam_ref, g_ref, o_ref, n_pairs, acc_ref):
    rows = acc_ref.shape[1]
    lane = lax.broadcasted_iota(I32, (rows, LANES), 1)
    for pp in range(n_pairs):
        a0, a1 = acc_ref[2 * pp], acc_ref[2 * pp + 1]
        if mode == "diff":
            lp = lam_ref[...]
            lam = (jnp.exp(jnp.sum(lp[0:1] * lp[1:2], axis=1, keepdims=True))
                   - jnp.exp(jnp.sum(lp[2:3] * lp[3:4], axis=1, keepdims=True)) + lam_init)
            o = a0[:, :LANES] / a0[:, LANES:] - lam * (a1[:, :LANES] / a1[:, LANES:])
            o = _rms(o, g_ref[...], SUBLN_EPS) * (1.0 - lam_init)
        else:
            o = jnp.where(lane < HEAD_DIM, a0 / pltpu.roll(a0, HEAD_DIM, 1), a1 / pltpu.roll(a1, HEAD_DIM, 1))
        o_ref[0, :, pp * LANES:(pp + 1) * LANES] = o.astype(o_ref.dtype)


def _attn_prompt_kernel(mode, tq, tk, nk, n_pairs, lam_init, *refs):
    q_ref, k_ref, v_ref = refs[:3]
    cq_ref = ck_ref = bias_ref = lam_ref = g_ref = None
    if mode == "fox":
        cq_ref, ck_ref = refs[3:5]
        rest = refs[5:]
    elif mode == "diff":
        lam_ref, g_ref = refs[3:5]
        rest = refs[5:]
    else:
        bias_ref = refs[3]
        rest = refs[4:]
    o_ref, qm_ref, m_ref, acc_ref = rest
    hp, qi, ki = pl.program_id(1), pl.program_id(2), pl.program_id(3)

    @pl.when(ki == 0)
    def _():
        _attn_start(q_ref, n_pairs, qm_ref, m_ref, acc_ref)

    q_lo, k_lo = qi * tq, ki * tk
    needed = k_lo <= q_lo + tq - 1
    straddle = k_lo + tk - 1 > q_lo

    def process(masked):
        keep = None
        if masked:
            rows = q_lo + lax.broadcasted_iota(I32, (tq, tk), 0)
            cols = k_lo + lax.broadcasted_iota(I32, (tq, tk), 1)
            keep = cols <= rows if mode == "fox" else (cols >> CHUNK_SHIFT) <= (rows >> CHUNK_SHIFT)
        bias = bias_ref[0, 0].astype(F32) if mode == "dsa" else None
        cq_blk = cq_ref[0] if mode == "fox" else None

        def get_kv(pp):
            sl = slice(pp * LANES, (pp + 1) * LANES)
            return k_ref[0, :, sl], v_ref[0, :, sl]

        _block_update(mode, n_pairs, hp * n_pairs, get_kv, keep, cq_blk, ck_ref, bias,
                      qm_ref, m_ref, acc_ref, PROMPT_DEPTH)

    if mode == "dsa":
        pl.when(needed)(lambda: process(False))
    else:
        pl.when(needed & straddle)(lambda: process(True))
        pl.when(needed & jnp.logical_not(straddle))(lambda: process(False))

    @pl.when(ki == nk - 1)
    def _():
        _attn_finalize(mode, lam_init, lam_ref, g_ref, o_ref, n_pairs, acc_ref)


def _attn_prompt(mode, q, k, v, extras, lam_init=0.0, tq_pref=512, tk_pref=512, n_pairs=4):
    b, t, d = q.shape
    tq, tk = _tile(t, tq_pref), _tile(t, tk_pref)
    nq, nk = t // tq, t // tk
    wl = LANES * n_pairs
    width = 2 * LANES if mode == "diff" else LANES

    def kmax(qi):
        return (qi * tq + tq - 1) // tk

    in_specs = [
        pl.BlockSpec((1, tq, wl), lambda bi, hp, qi, ki: (bi, qi, hp)),
        pl.BlockSpec((1, tk, wl), lambda bi, hp, qi, ki: (bi, jnp.minimum(ki, kmax(qi)), hp)),
        pl.BlockSpec((1, tk, wl), lambda bi, hp, qi, ki: (bi, jnp.minimum(ki, kmax(qi)), hp)),
    ]
    if mode == "fox":
        nh = extras[0].shape[2]
        in_specs += [
            pl.BlockSpec((1, tq, nh), lambda bi, hp, qi, ki: (bi, qi, 0)),
            pl.BlockSpec((1, nh, tk), lambda bi, hp, qi, ki: (bi, 0, jnp.minimum(ki, kmax(qi)))),
        ]
    elif mode == "diff":
        in_specs += [
            pl.BlockSpec((4, HEAD_DIM), lambda bi, hp, qi, ki: (0, 0)),
            pl.BlockSpec((1, LANES), lambda bi, hp, qi, ki: (0, 0)),
        ]
    else:
        in_specs += [
            pl.BlockSpec((1, 1, tq, tk), lambda bi, hp, qi, ki: (bi, jnp.minimum(ki, kmax(qi)), qi, 0)),
        ]
    return pl.pallas_call(
        functools.partial(_attn_prompt_kernel, mode, tq, tk, nk, n_pairs, lam_init),
        grid=(b, d // wl, nq, nk),
        in_specs=in_specs,
        out_specs=pl.BlockSpec((1, tq, wl), lambda bi, hp, qi, ki: (bi, qi, hp)),
        out_shape=jax.ShapeDtypeStruct((b, t, d), BF16),
        scratch_shapes=_attn_scratch(2 * n_pairs, tq, width),
        compiler_params=_cparams("parallel", "parallel", "parallel", "arbitrary"),
        name="attn_prompt_" + mode,
    )(q, k, v, *extras)


def _attn_sample_kernel(mode, ts, tk, nkp, n_pairs, lam_init, *refs):
    q_ref, pk_ref, pv_ref, nk_ref, nv_ref = refs[:5]
    cq_ref = ckp_ref = ckn_ref = bp_ref = bn_ref = lam_ref = g_ref = None
    if mode == "fox":
        cq_ref, ckp_ref, ckn_ref = refs[5:8]
        rest = refs[8:]
    elif mode == "diff":
        lam_ref, g_ref = refs[5:7]
        rest = refs[7:]
    else:
        bp_ref, bn_ref = refs[5:7]
        rest = refs[7:]
    o_ref, qm_ref, m_ref, acc_ref = rest
    ki = pl.program_id(1)

    @pl.when(ki == 0)
    def _():
        _attn_start(q_ref, n_pairs, qm_ref, m_ref, acc_ref)

    cq_blk = cq_ref[0] if mode == "fox" else None
    bias = bp_ref[0, 0].astype(F32) if mode == "dsa" else None

    def past_kv(pp):
        sl = slice(pp * LANES, (pp + 1) * LANES)
        return pk_ref[0, 0, :, sl].astype(BF16), pv_ref[0, 0, :, sl].astype(BF16)

    _block_update(mode, n_pairs, 0, past_kv, None, cq_blk, ckp_ref, bias, qm_ref, m_ref, acc_ref, SAMPLE_DEPTH)

    @pl.when(ki == nkp - 1)
    def _():
        keep = None
        if mode == "fox":
            keep = lax.broadcasted_iota(I32, (ts, ts), 1) <= lax.broadcasted_iota(I32, (ts, ts), 0)
        bias_n = bn_ref[0, 0][:, :ts].astype(F32) if mode == "dsa" else None

        def new_kv(pp):
            sl = slice(pp * LANES, (pp + 1) * LANES)
            return nk_ref[0, :, sl], nv_ref[0, :, sl]

        _block_update(mode, n_pairs, 0, new_kv, keep, cq_blk, ckn_ref, bias_n, qm_ref, m_ref, acc_ref, SAMPLE_DEPTH)
        _attn_finalize(mode, lam_init, lam_ref, g_ref, o_ref, n_pairs, acc_ref)


def _attn_sample(mode, layer, q, past_k, past_v, new_k, new_v, extras, lam_init=0.0, tk_pref=512):
    b, ts, d = q.shape
    p = past_k.shape[2]
    tk = _tile(p, tk_pref)
    nkp = p // tk
    n_pairs = d // LANES
    width = 2 * LANES if mode == "diff" else LANES
    in_specs = [
        pl.BlockSpec((1, ts, d), lambda bi, ki: (bi, 0, 0)),
        pl.BlockSpec((1, 1, tk, d), lambda bi, ki: (layer, bi, ki, 0)),
        pl.BlockSpec((1, 1, tk, d), lambda bi, ki: (layer, bi, ki, 0)),
        pl.BlockSpec((1, ts, d), lambda bi, ki: (bi, 0, 0)),
        pl.BlockSpec((1, ts, d), lambda bi, ki: (bi, 0, 0)),
    ]
    if mode == "fox":
        nh = extras[0].shape[2]
        in_specs += [
            pl.BlockSpec((1, ts, nh), lambda bi, ki: (bi, 0, 0)),
            pl.BlockSpec((1, nh, tk), lambda bi, ki: (bi, 0, ki)),
            pl.BlockSpec((1, nh, LANES), lambda bi, ki: (bi, 0, p // LANES)),
        ]
    elif mode == "diff":
        in_specs += [
            pl.BlockSpec((4, HEAD_DIM), lambda bi, ki: (0, 0)),
            pl.BlockSpec((1, LANES), lambda bi, ki: (0, 0)),
        ]
    else:
        in_specs += [
            pl.BlockSpec((1, 1, ts, tk), lambda bi, ki: (bi, ki, 0, 0)),
            pl.BlockSpec((1, 1, ts, tk), lambda bi, ki: (bi, nkp, 0, 0)),
        ]
    return pl.pallas_call(
        functools.partial(_attn_sample_kernel, mode, ts, tk, nkp, n_pairs, lam_init),
        grid=(b, nkp),
        in_specs=in_specs,
        out_specs=pl.BlockSpec((1, ts, d), lambda bi, ki: (bi, 0, 0)),
        out_shape=jax.ShapeDtypeStruct((b, ts, d), BF16),
        scratch_shapes=_attn_scratch(2 * n_pairs, ts, width),
        compiler_params=_cparams("parallel", "arbitrary"),
        name="attn_sample_" + mode,
    )(q, past_k, past_v, new_k, new_v, *extras)


def _dsa_select_kernel(tq, tk, nkb, top_k, q_pos0, n_valid, causal, qi_ref, w_ref, kk_ref, tri_ref, bias_ref, key_ref):
    i = pl.program_id(1)
    nb = (i * tq + tq - 1) // tk + 1 if causal else nkb
    q_chunk = (q_pos0 + i * tq + lax.broadcasted_iota(I32, (tq, tk), 0)) >> CHUNK_SHIFT
    col = lax.broadcasted_iota(I32, (tq, tk), 1)

    lane = lax.broadcasted_iota(I32, (tq, LANES), 1)
    q_rows = []
    for h in range(N_IDX_HEADS):
        q2 = qi_ref[0, :, (h // 2) * LANES:(h // 2 + 1) * LANES]
        q_rows.append(jnp.where((lane >= IDX_DIM) if h % 2 else (lane < IDX_DIM), q2, jnp.zeros_like(q2)))
    q_all = jnp.concatenate(q_rows, axis=0)

    def score_block(kb, carry):
        k2 = kk_ref[0, pl.ds(pl.multiple_of(kb * tk, tk), tk), :].astype(BF16)
        s_all = _qk(q_all, k2)
        sc = jnp.zeros((tq, tk), F32)
        for h in range(N_IDX_HEADS):
            sc = sc + jnp.maximum(s_all[h * tq:(h + 1) * tq], 0.0) * w_ref[0, :, IDX_DIM + h:IDX_DIM + h + 1]
        sc = jnp.where(sc == 0.0, 0.0, sc)
        kidx = kb * tk + col
        adm = ((kidx >> CHUNK_SHIFT) <= q_chunk) & (kidx < n_valid)
        bits = pltpu.bitcast(sc, I32)
        key = jnp.where(bits >= 0, bits, bits ^ jnp.int32(0x7FFFFFFF))
        key_ref[kb] = jnp.where(adm, key, jnp.int32(INT_MIN))
        return carry

    lax.fori_loop(0, nb, score_block, 0)

    def count(pred):
        def body(kb, acc):
            key = key_ref[kb]
            for c in range(tk // LANES):
                acc = acc + jnp.where(pred(key[:, c * LANES:(c + 1) * LANES]), 1.0, 0.0)
            return acc
        acc = lax.fori_loop(0, nb, body, jnp.zeros((tq, LANES), F32))
        return jnp.sum(acc, axis=1, keepdims=True)

    def bit_body(it, u):
        cand = u | lax.shift_left(jnp.int32(1), 31 - it)
        cand_b = jnp.broadcast_to(cand ^ jnp.int32(INT_MIN), (tq, LANES))
        cnt = count(lambda key: key >= cand_b)
        return jnp.where(cnt >= top_k, cand, u)

    u = lax.fori_loop(0, 32, bit_body, jnp.zeros((tq, 1), I32))
    thr = u ^ jnp.int32(INT_MIN)
    floor_b = jnp.broadcast_to(jnp.maximum(thr, jnp.int32(INT_MIN + 1)), (tq, LANES))
    n_ge = count(lambda key: key >= floor_b)
    tied = jnp.max(n_ge) > top_k

    @pl.when(jnp.logical_not(tied))
    def _():
        floor = jnp.maximum(thr, jnp.int32(INT_MIN + 1))

        def out_body(kb, carry):
            bias_ref[0, kb] = jnp.where(key_ref[kb] >= floor, 0.0, NEG).astype(bias_ref.dtype)
            return carry

        lax.fori_loop(0, nb, out_body, 0)

    @pl.when(tied)
    def _():
        thr_b = jnp.broadcast_to(thr, (tq, LANES))
        need = top_k - count(lambda key: key > thr_b)

        def out_body(kb, seen):
            key = key_ref[kb]
            eq = (key == thr) & (key > jnp.int32(INT_MIN))
            eqf = jnp.where(eq, 1.0, 0.0)
            rank = seen + jnp.dot(eqf.astype(BF16), tri_ref[...], preferred_element_type=F32)
            sel = (key > thr) | (eq & (rank < need))
            bias_ref[0, kb] = jnp.where(sel, 0.0, NEG).astype(bias_ref.dtype)
            return seen + jnp.sum(eqf, axis=1, keepdims=True)

        lax.fori_loop(0, nb, out_body, jnp.zeros((tq, 1), F32))

    def fill(kb, carry):
        bias_ref[0, kb] = jnp.full((tq, tk), NEG, bias_ref.dtype)
        return carry

    lax.fori_loop(nb, nkb, fill, 0)


def _dsa_select(qidx, wsrc, keys, q_pos0, n_valid, causal, tq, tk):
    b, t, _ = qidx.shape
    s, kd = keys.shape[1], keys.shape[2]
    nkb = s // tk
    top_k = min(TOPK_MAX, n_valid // 4)
    tri = (jnp.arange(tk)[:, None] < jnp.arange(tk)[None, :]).astype(BF16)
    return pl.pallas_call(
        functools.partial(_dsa_select_kernel, tq, tk, nkb, top_k, q_pos0, n_valid, causal),
        grid=(b, t // tq),
        in_specs=[
            pl.BlockSpec((1, tq, N_IDX_HEADS * IDX_DIM), lambda bi, i: (bi, i, 0)),
            pl.BlockSpec((1, tq, LANES), lambda bi, i: (bi, i, 0)),
            pl.BlockSpec((1, s, kd), lambda bi, i: (bi, 0, 0)),
            pl.BlockSpec((tk, tk), lambda bi, i: (0, 0)),
        ],
        out_specs=pl.BlockSpec((1, nkb, tq, tk), lambda bi, i: (bi, 0, i, 0)),
        out_shape=jax.ShapeDtypeStruct((b, nkb, t, tk), BF16),
        scratch_shapes=[pltpu.VMEM((nkb, tq, tk), I32)],
        compiler_params=_cparams("parallel", "arbitrary"),
        name="dsa_select",
    )(qidx, wsrc, keys, tri)


def _pad_cols(w, n):
    return jnp.pad(w, ((0, 0), (0, n - w.shape[1])))


def _diff_lambda_init(layer):
    return 0.8 - 0.6 * math.exp(-0.3 * layer)


def _fox_layer(j, g, hp, hs, bp, bs, cache_k, cache_v, cache_logf, w_qkv, w_f, b_f, w_o, tabs_p, tabs_s):
    d = D_MODEL
    nh = w_f.shape[1]
    w = w_qkv.astype(BF16)
    outs = [(0, d, False, Q_SCALE, BF16), (d, d, False, 1.0, F32), (2 * d, d, False, 1.0, F32),
            (d, d, False, 1.0, BF16), (2 * d, d, False, 1.0, BF16)]
    wf = _pad_cols(w_f, LANES)
    bf = _pad_cols(b_f.reshape(1, nh), LANES)
    res = []
    for h, nb, tabs in ((hp, bp, tabs_p), (hs, bs, tabs_s)):
        q, k, v, kb, vb = (a.reshape(nb, -1, d) for a in _proj(h, g, w, outs, *tabs))
        logf = _small_proj("fgate", h, g, wf, bf, *tabs)[:, :nh]
        res.append((q, k, v, kb, vb, logf.reshape(nb, -1, nh)))
    (qp, kp, vp, kbp, vbp, lfp), (qs, ks, vs, kbs, vbs, lfs) = res

    cum_row = _cumsum_time(jnp.swapaxes(lfp, 1, 2))
    op = _attn_prompt("fox", qp, kbp, vbp, (jnp.swapaxes(cum_row, 1, 2), cum_row))

    past = cache_logf.shape[2]
    ts = lfs.shape[1]
    lf_all = jnp.swapaxes(jnp.concatenate([cache_logf[j], lfs], axis=1), 1, 2)
    padded = -(-(past + ts) // LANES) * LANES
    cum_s = _cumsum_time(jnp.pad(lf_all, ((0, 0), (0, 0), (0, padded - past - ts))))
    cq_s = jnp.swapaxes(cum_s[:, :, past:past + ts], 1, 2)
    os_ = _attn_sample("fox", j, qs, cache_k, cache_v, kbs, vbs, (cq_s, cum_s, cum_s))

    wo = w_o.astype(BF16)
    hp = _wo(op.reshape(-1, d), wo, hp)
    hs = _wo(os_.reshape(-1, d), wo, hs)
    return hp, hs, (kp, vp, lfp), (ks, vs, lfs)


def _diff_layer(layer, j, g, hp, hs, bp, bs, cache_k, cache_v, w_qkv, lams, subln_g, w_o, tabs_p, tabs_s):
    d = D_MODEL
    w = w_qkv.astype(BF16)
    outs = [(0, d, True, Q_SCALE, BF16), (d, d, True, 1.0, F32), (2 * d, d, False, 1.0, F32),
            (d, d, True, 1.0, BF16), (2 * d, d, False, 1.0, BF16)]
    lam_init = _diff_lambda_init(layer)
    extras = (lams, subln_g.reshape(1, LANES))
    qp, kp, vp, kbp, vbp = (a.reshape(bp, -1, d) for a in _proj(hp, g, w, outs, *tabs_p))
    qs, ks, vs, kbs, vbs = (a.reshape(bs, -1, d) for a in _proj(hs, g, w, outs, *tabs_s))
    op = _attn_prompt("diff", qp, kbp, vbp, extras, lam_init)
    os_ = _attn_sample("diff", j, qs, cache_k, cache_v, kbs, vbs, extras, lam_init)
    wo = w_o.astype(BF16)
    hp = _wo(op.reshape(-1, d), wo, hp)
    hs = _wo(os_.reshape(-1, d), wo, hs)
    return hp, hs, (kp, vp), (ks, vs)


def _dsa_layer(j, g, hp, hs, bp, bs, cache_k, cache_v, cache_kidx, w_qkv, w_qidx, w_kidx, w_widx, w_o, tabs_p, tabs_s):
    d = D_MODEL
    ni = N_IDX_HEADS * IDX_DIM
    w = jnp.concatenate([w_qkv, w_qidx], axis=1).astype(BF16)
    outs = [(0, d, True, Q_SCALE, BF16), (d, d, True, 1.0, F32), (2 * d, d, False, 1.0, F32),
            (3 * d, ni, True, IDX_SCALE, BF16), (d, d, True, 1.0, BF16), (2 * d, d, False, 1.0, BF16)]
    w_small = _pad_cols(jnp.concatenate([w_kidx, w_widx], axis=1), LANES)
    zero_b = jnp.zeros((1, LANES), F32)

    qp, kp, vp, qip, kbp, vbp = (a.reshape(bp, -1, a.shape[-1]) for a in _proj(hp, g, w, outs, *tabs_p))
    spp = _small_proj("dsa", hp, g, w_small, zero_b, *tabs_p).reshape(bp, -1, LANES)
    t = qp.shape[1]
    sel_tk = _tile(t, 512)
    keys_p = jnp.concatenate([spp[:, :, :IDX_DIM]] * 2, axis=-1)
    bias_p = _dsa_select(qip, spp, keys_p, 0, t, True, _tile(t, 128), sel_tk)
    op = _attn_prompt("dsa", qp, kbp, vbp, (bias_p,), tq_pref=sel_tk, tk_pref=sel_tk)

    qs, ks, vs, qis, kbs, vbs = (a.reshape(bs, -1, a.shape[-1]) for a in _proj(hs, g, w, outs, *tabs_s))
    sps = _small_proj("dsa", hs, g, w_small, zero_b, *tabs_s).reshape(bs, -1, LANES)
    ts = qs.shape[1]
    past = cache_kidx.shape[2]
    tk_s = _tile(past, 512)
    keys_s = jnp.concatenate([cache_kidx[j], sps[:, :, :IDX_DIM]], axis=1)
    keys_s = jnp.pad(jnp.concatenate([keys_s, keys_s], axis=-1), ((0, 0), (0, tk_s - ts), (0, 0)))
    bias_s = _dsa_select(qis, sps, keys_s, past, past + ts, False, ts, tk_s)
    os_ = _attn_sample("dsa", j, qs, cache_k, cache_v, kbs, vbs, (bias_s, bias_s), tk_pref=tk_s)

    wo = w_o.astype(BF16)
    hp = _wo(op.reshape(-1, d), wo, hp)
    hs = _wo(os_.reshape(-1, d), wo, hs)
    return hp, hs, (kp, vp, spp[:, :, :IDX_DIM]), (ks, vs, sps[:, :, :IDX_DIM])


def kernel(x_prompt, x_sample, cache_a_k, cache_a_v, cache_a_logf, cache_b_k, cache_b_v, cache_c_k, cache_c_v, cache_c_kidx, norm_mix_g, norm_ffn_g, norm_final_g, a_w_qkv, a_w_f, a_b_f, a_w_o, b_w_qkv, b_lam_q1, b_lam_k1, b_lam_q2, b_lam_k2, b_subln_g, b_w_o, c_w_qkv, c_w_qidx, c_w_kidx, c_w_widx, c_w_o, ffn_w_up, ffn_w_down):
    bp, tp, d = x_prompt.shape
    bs, ts, _ = x_sample.shape
    past = cache_a_k.shape[2]
    depth = norm_mix_g.shape[0]
    tabs_p = _rope_tables(jnp.arange(tp, dtype=I32))
    cos_s, sin_s = _rope_tables(past + jnp.arange(ts, dtype=I32))
    tabs_s = (jnp.tile(cos_s, (bs, 1)), jnp.tile(sin_s, (bs, 1)))

    cache_a_k, cache_a_v, cache_b_k, cache_b_v, cache_c_k, cache_c_v = (
        c.reshape(*c.shape[:3], d) for c in (cache_a_k, cache_a_v, cache_b_k, cache_b_v, cache_c_k, cache_c_v))

    hp = x_prompt.reshape(bp * tp, d)
    hs = x_sample.reshape(bs * ts, d)
    new = {"a": ([], []), "b": ([], []), "c": ([], [])}
    for i in range(depth):
        kind, j = i % N_MIXERS, i // N_MIXERS
        g = norm_mix_g[i]
        if kind == 0:
            hp, hs, sp, ss = _fox_layer(j, g, hp, hs, bp, bs, cache_a_k, cache_a_v, cache_a_logf,
                                        a_w_qkv[j], a_w_f[j], a_b_f[j], a_w_o[j], tabs_p, tabs_s)
            key = "a"
        elif kind == 1:
            lams = jnp.stack([b_lam_q1[j], b_lam_k1[j], b_lam_q2[j], b_lam_k2[j]])
            hp, hs, sp, ss = _diff_layer(i, j, g, hp, hs, bp, bs, cache_b_k, cache_b_v,
                                         b_w_qkv[j], lams, b_subln_g[j], b_w_o[j], tabs_p, tabs_s)
            key = "b"
        else:
            hp, hs, sp, ss = _dsa_layer(j, g, hp, hs, bp, bs, cache_c_k, cache_c_v, cache_c_kidx,
                                        c_w_qkv[j], c_w_qidx[j], c_w_kidx[j], c_w_widx[j], c_w_o[j], tabs_p, tabs_s)
            key = "c"
        new[key][0].append(sp)
        new[key][1].append(ss)
        wu, wd = ffn_w_up[i].astype(BF16), ffn_w_down[i].astype(BF16)
        hp = _ffn(hp, norm_ffn_g[i], wu, wd)
        hs = _ffn(hs, norm_ffn_g[i], wu, wd)

    y_prompt = _final_norm(hp, norm_final_g).reshape(bp, tp, d)
    y_sample = _final_norm(hs, norm_final_g).reshape(bs, ts, d)

    def stack(items, idx, shape_tail):
        return jnp.stack([s[idx].reshape(*s[idx].shape[:2], *shape_tail) for s in items])

    nha = d // HEAD_DIM
    outs = [y_prompt, y_sample]
    for side in (0, 1):
        a, b, c = new["a"][side], new["b"][side], new["c"][side]
        outs += [
            stack(a, 0, (nha, HEAD_DIM)), stack(a, 1, (nha, HEAD_DIM)), stack(a, 2, (nha,)),
            stack(b, 0, (nha, HEAD_DIM)), stack(b, 1, (nha // 2, 2 * HEAD_DIM)),
            stack(c, 0, (nha, HEAD_DIM)), stack(c, 1, (nha, HEAD_DIM)), stack(c, 2, (IDX_DIM,)),
        ]
    return tuple(outs)
```
